```python
import math
import jax, jax.numpy as jnp
from jax import lax
import numpy as np

D_MODEL = 1024
BATCH = 8
SEQ = 4096
DEPTH = 1
DEC_BATCH = 128
DEC_SEQ = 8
PAST_LEN = 8192
PAGE_SIZE = 128

HEAD_DIM = 64
N_HEADS_A = 12
DILATED_BRANCHES = ((128, 1), (512, 4), (2048, 16))
WIN_MAX = 2048
Q_BLOCK = 128
N_BUCKETS = 32
BUCKET_MAX_DIST = 2048
N_HEADS_B = 4
DK_B = 32
DV_B = 64
GATE_RANK = 16
GATE_NORM = 16.0
GLA_CHUNK = 16
WIDTH_A = N_HEADS_A * HEAD_DIM
WIDTH_BK = N_HEADS_B * DK_B
WIDTH_BV = N_HEADS_B * DV_B
MIX_WIDTH = WIDTH_A + WIDTH_BV
PROJ_SPLITS = (WIDTH_A, WIDTH_A, WIDTH_A, WIDTH_BK, WIDTH_BK, WIDTH_BV, WIDTH_BV, GATE_RANK)
PROJ_WIDTH = sum(PROJ_SPLITS)
D_FF = 2816
EPS = 1e-6

kernel_name = 'hybrid_dilated_gla_macaron_step'


def _rmsnorm(x, gain):
    xf = x.astype(jnp.float32)
    y = xf * lax.rsqrt(jnp.mean(xf * xf, axis=-1, keepdims=True) + EPS)
    return (y * gain.astype(jnp.float32)).astype(x.dtype)


def _half_ffn(x, gain, w1, w3, w2):
    h = _rmsnorm(x, gain)
    return x + 0.5 * ((jax.nn.silu(h @ w1) * (h @ w3)) @ w2)


def _t5_bucket(dist):
    max_exact = N_BUCKETS // 2
    d = jnp.maximum(dist, 1).astype(jnp.float32)
    large = max_exact + (jnp.log(d / max_exact) / math.log(BUCKET_MAX_DIST / max_exact)
                         * (N_BUCKETS - max_exact)).astype(jnp.int32)
    large = jnp.minimum(large, N_BUCKETS - 1)
    return jnp.where(dist < max_exact, dist, large)


def _project(h, w_in, q_gain, k_gain, w_gk2, b_gk):
    B, S, _ = h.shape
    z = h @ w_in
    qa, ka, va, qb, kb, vb, rb, glr = jnp.split(z, np.cumsum(PROJ_SPLITS)[:-1].tolist(), axis=-1)
    qa = _rmsnorm(qa.reshape(B, S, N_HEADS_A, HEAD_DIM), q_gain)
    ka = _rmsnorm(ka.reshape(B, S, N_HEADS_A, HEAD_DIM), k_gain)
    va = va.reshape(B, S, N_HEADS_A, HEAD_DIM)
    qb = qb.reshape(B, S, N_HEADS_B, DK_B) * (DK_B ** -0.5)
    kb = kb.reshape(B, S, N_HEADS_B, DK_B)
    vb = vb.reshape(B, S, N_HEADS_B, DV_B)
    gk = jax.nn.log_sigmoid((glr @ w_gk2 + b_gk).astype(jnp.float32)) / GATE_NORM
    gk = gk.reshape(B, S, N_HEADS_B, DK_B)
    return qa, ka, va, qb, kb, vb, rb, gk


def _dilated_prompt(q, k, v, rel_bias, window, dilation):
    B, S, H, Dh = q.shape
    L = S // dilation
    nk = window // dilation
    bq = math.gcd(L, Q_BLOCK)
    nb = L // bq

    def split(t):
        return t.reshape(B, L, dilation, H, Dh).transpose(0, 2, 1, 3, 4)

    qs, ks, vs = split(q), split(k), split(v)
    pad = ((0, 0), (0, 0), (nk, 0), (0, 0), (0, 0))
    kp, vp = jnp.pad(ks, pad), jnp.pad(vs, pad)
    kidx = jnp.arange(nb)[:, None] * bq + jnp.arange(bq + nk)[None, :]
    kb, vb = kp[:, :, kidx], vp[:, :, kidx]
    qb = qs.reshape(B, dilation, nb, bq, H, Dh)
    logits = jnp.einsum('brnqhd,brnkhd->brnhqk', qb, kb,
                        preferred_element_type=jnp.float32) * (Dh ** -0.5)
    i = jnp.arange(bq)[:, None]
    j = jnp.arange(bq + nk)[None, :]
    step = i - j + nk
    kpos = jnp.arange(nb)[:, None, None] * bq + j[None] - nk
    valid = (step >= 0) & (step <= nk) & (kpos >= 0)
    bias = rel_bias[_t5_bucket(jnp.clip(step, 0, nk) * dilation)].astype(jnp.float32)
    logits = logits + bias.transpose(2, 0, 1)
    logits = jnp.where(valid[None, None, :, None], logits, -jnp.inf)
    lse = jax.nn.logsumexp(logits, axis=-1)
    p = jnp.exp(logits - lse[..., None])
    o = jnp.einsum('brnhqk,brnkhd->brnqhd', p.astype(vb.dtype), vb,
                   preferred_element_type=jnp.float32)
    o = o.reshape(B, dilation, L, H, Dh).transpose(0, 2, 1, 3, 4).reshape(B, S, H, Dh)
    lse = lse.transpose(0, 1, 2, 4, 3).reshape(B, dilation, L, H).transpose(0, 2, 1, 3).reshape(B, S, H)
    return o, lse


def _dilated_sample(q, k_all, v_all, rel_bias, window, dilation, w_buf):
    T, Dh = q.shape[1], q.shape[-1]
    nk = window // dilation
    steps = jnp.arange(nk + 1)
    idx = (w_buf + jnp.arange(T))[:, None] - steps[None, :] * dilation
    valid = idx >= 0
    idx_c = jnp.maximum(idx, 0)
    kg, vg = k_all[:, idx_c], v_all[:, idx_c]
    logits = jnp.einsum('bthd,btkhd->bhtk', q, kg,
                        preferred_element_type=jnp.float32) * (Dh ** -0.5)
    bias = rel_bias[_t5_bucket(steps * dilation)].astype(jnp.float32)
    logits = logits + bias.T[None, :, None, :]
    logits = jnp.where(valid[None, None], logits, -jnp.inf)
    lse = jax.nn.logsumexp(logits, axis=-1)
    p = jnp.exp(logits - lse[..., None])
    o = jnp.einsum('bhtk,btkhd->bthd', p.astype(vg.dtype), vg,
                   preferred_element_type=jnp.float32)
    return o, lse.transpose(0, 2, 1)


def _merge_branches(outs, lses):
    w = jax.nn.softmax(jnp.stack(lses, 0), axis=0)
    return jnp.einsum('nbsh,nbshd->bshd', w, jnp.stack(outs, 0))


def _gla(q, k, v, g, s0, chunk):
    B, S, H, Dk = q.shape
    Dv = v.shape[-1]
    n = S // chunk
    f32 = jnp.float32
    qc = q.astype(f32).reshape(B, n, chunk, H, Dk)
    kc = k.astype(f32).reshape(B, n, chunk, H, Dk)
    vc = v.astype(f32).reshape(B, n, chunk, H, Dv)
    b = jnp.cumsum(g.astype(f32).reshape(B, n, chunk, H, Dk), axis=2)
    t_idx = jnp.arange(chunk)
    causal = (t_idx[:, None] >= t_idx[None, :])[None, None, :, :, None, None]
    diff = b[:, :, :, None] - b[:, :, None, :]
    decay = jnp.exp(jnp.where(causal, diff, -jnp.inf))
    scores = jnp.einsum('bnthk,bnshk,bntshk->bnhts', qc, kc, decay)
    o_intra = jnp.einsum('bnhts,bnshv->bnthv', scores, vc)
    b_last = b[:, :, -1]
    k_dec = kc * jnp.exp(b_last[:, :, None] - b)
    ds = jnp.einsum('bnshk,bnshv->bnhkv', k_dec, vc)

    def step(state, inp):
        dec, d_state = inp
        return jnp.exp(dec)[..., None] * state + d_state, state

    s_fin, s_start = lax.scan(step, s0.astype(f32),
                              (jnp.moveaxis(b_last, 1, 0), jnp.moveaxis(ds, 1, 0)))
    s_start = jnp.moveaxis(s_start, 0, 1)
    o_inter = jnp.einsum('bnthk,bnhkv->bnthv', qc * jnp.exp(b), s_start)
    return (o_intra + o_inter).reshape(B, S, H, Dv), s_fin


def _merge_out(oa, ob, rb, gla_gain, w_out):
    B, S = oa.shape[:2]
    ob = _rmsnorm(ob, gla_gain).reshape(B, S, WIDTH_BV) * jax.nn.silu(rb.astype(jnp.float32))
    cat = jnp.concatenate([oa.reshape(B, S, WIDTH_A), ob], axis=-1).astype(w_out.dtype)
    return cat @ w_out


def _mix_prompt(h, w_in, q_gain, k_gain, rel_bias, w_gk2, b_gk, gla_gain, w_out):
    B, S, _ = h.shape
    qa, ka, va, qb, kb, vb, rb, gk = _project(h, w_in, q_gain, k_gain, w_gk2, b_gk)
    outs, lses = [], []
    for window, dil in DILATED_BRANCHES:
        o, l = _dilated_prompt(qa, ka, va, rel_bias, window, dil)
        outs.append(o)
        lses.append(l)
    oa = _merge_branches(outs, lses)
    s0 = jnp.zeros((B, N_HEADS_B, DK_B, DV_B), jnp.float32)
    ob, s_fin = _gla(qb, kb, vb, gk, s0, math.gcd(S, GLA_CHUNK))
    y = _merge_out(oa, ob, rb, gla_gain, w_out)
    n_keep = min(WIN_MAX, S)
    return y, ka[:, S - n_keep:], va[:, S - n_keep:], s_fin


def _mix_sample(h, win_k, win_v, gla_state, w_in, q_gain, k_gain, rel_bias, w_gk2, b_gk, gla_gain, w_out):
    T = h.shape[1]
    w_buf = win_k.shape[1]
    qa, ka, va, qb, kb, vb, rb, gk = _project(h, w_in, q_gain, k_gain, w_gk2, b_gk)
    k_all = jnp.concatenate([win_k, ka.astype(win_k.dtype)], axis=1)
    v_all = jnp.concatenate([win_v, va.astype(win_v.dtype)], axis=1)
    outs, lses = [], []
    for window, dil in DILATED_BRANCHES:
        o, l = _dilated_sample(qa, k_all, v_all, rel_bias, window, dil, w_buf)
        outs.append(o)
        lses.append(l)
    oa = _merge_branches(outs, lses)
    ob, s_new = _gla(qb, kb, vb, gk, gla_state, math.gcd(T, GLA_CHUNK))
    y = _merge_out(oa, ob, rb, gla_gain, w_out)
    return y, k_all[:, T:], v_all[:, T:], s_new


def setup_inputs(seed: int = 0) -> dict:
    key = jax.random.key(seed)
    ks = jax.random.split(key, 24)
    f32 = jnp.float32

    def nrm(k, shape, scale):
        return jax.random.normal(k, shape, f32) * scale

    w_buf = min(WIN_MAX, PAST_LEN)
    return {
        'x_prompt': nrm(ks[0], (BATCH, SEQ, D_MODEL), 1.0),
        'x_sample': nrm(ks[1], (DEC_BATCH, DEC_SEQ, D_MODEL), 1.0),
        'cache_win_k': nrm(ks[2], (DEPTH, DEC_BATCH, w_buf, N_HEADS_A, HEAD_DIM), 1.0),
        'cache_win_v': nrm(ks[3], (DEPTH, DEC_BATCH, w_buf, N_HEADS_A, HEAD_DIM), 1.0),
        'state_gla': nrm(ks[4], (DEPTH, DEC_BATCH, N_HEADS_B, DK_B, DV_B), 0.5),
        'ffn1_norm': 1.0 + nrm(ks[5], (DEPTH, D_MODEL), 0.02),
        'ffn1_w1': nrm(ks[6], (DEPTH, D_MODEL, D_FF), D_MODEL ** -0.5),
        'ffn1_w3': nrm(ks[7], (DEPTH, D_MODEL, D_FF), D_MODEL ** -0.5),
        'ffn1_w2': nrm(ks[8], (DEPTH, D_FF, D_MODEL), D_FF ** -0.5),
        'mix_norm': 1.0 + nrm(ks[9], (DEPTH, D_MODEL), 0.02),
        'w_in': nrm(ks[10], (DEPTH, D_MODEL, PROJ_WIDTH), D_MODEL ** -0.5),
        'q_norm': 1.0 + nrm(ks[11], (DEPTH, HEAD_DIM), 0.02),
        'k_norm': 1.0 + nrm(ks[12], (DEPTH, HEAD_DIM), 0.02),
        'rel_bias': nrm(ks[13], (N_BUCKETS, N_HEADS_A), 0.2),
        'w_gk2': nrm(ks[14], (DEPTH, GATE_RANK, WIDTH_BK), GATE_RANK ** -0.5),
        'b_gk': nrm(ks[15], (DEPTH, WIDTH_BK), 0.1),
        'gla_norm': 1.0 + nrm(ks[16], (DEPTH, DV_B), 0.02),
        'w_out': nrm(ks[17], (DEPTH, MIX_WIDTH, D_MODEL), MIX_WIDTH ** -0.5),
        'ffn2_norm': 1.0 + nrm(ks[18], (DEPTH, D_MODEL), 0.02),
        'ffn2_w1': nrm(ks[19], (DEPTH, D_MODEL, D_FF), D_MODEL ** -0.5),
        'ffn2_w3': nrm(ks[20], (DEPTH, D_MODEL, D_FF), D_MODEL ** -0.5),
        'ffn2_w2': nrm(ks[21], (DEPTH, D_FF, D_MODEL), D_FF ** -0.5),
    }


def reference(x_prompt, x_sample, cache_win_k, cache_win_v, state_gla,
              ffn1_norm, ffn1_w1, ffn1_w3, ffn1_w2, mix_norm, w_in, q_norm, k_norm,
              rel_bias, w_gk2, b_gk, gla_norm, w_out, ffn2_norm, ffn2_w1, ffn2_w3, ffn2_w2):
    yp, ys = x_prompt, x_sample
    kp_l, vp_l, sp_l, ks_l, vs_l, ss_l = [], [], [], [], [], []
    for l in range(DEPTH):
        yp = _half_ffn(yp, ffn1_norm[l], ffn1_w1[l], ffn1_w3[l], ffn1_w2[l])
        m, kp, vp, sp = _mix_prompt(_rmsnorm(yp, mix_norm[l]), w_in[l], q_norm[l], k_norm[l],
                                    rel_bias, w_gk2[l], b_gk[l], gla_norm[l], w_out[l])
        yp = yp + m.astype(yp.dtype)
        yp = _half_ffn(yp, ffn2_norm[l], ffn2_w1[l], ffn2_w3[l], ffn2_w2[l])
        ys = _half_ffn(ys, ffn1_norm[l], ffn1_w1[l], ffn1_w3[l], ffn1_w2[l])
        m, kn, vn, sn = _mix_sample(_rmsnorm(ys, mix_norm[l]), cache_win_k[l], cache_win_v[l],
                                    state_gla[l], w_in[l], q_norm[l], k_norm[l], rel_bias,
                                    w_gk2[l], b_gk[l], gla_norm[l], w_out[l])
        ys = ys + m.astype(ys.dtype)
        ys = _half_ffn(ys, ffn2_norm[l], ffn2_w1[l], ffn2_w3[l], ffn2_w2[l])
        kp_l.append(kp); vp_l.append(vp); sp_l.append(sp)
        ks_l.append(kn); vs_l.append(vn); ss_l.append(sn)
    return (yp, ys, jnp.stack(kp_l), jnp.stack(vp_l), jnp.stack(sp_l),
            jnp.stack(ks_l), jnp.stack(vs_l), jnp.stack(ss_l))
```

```python
import functools
import math

import jax
import jax.numpy as jnp
from jax import lax
from jax.experimental import pallas as pl
from jax.experimental.pallas import tpu as pltpu

F32 = jnp.float32
BF16 = jnp.bfloat16

HEAD_DIM = 64
N_HEADS_A = 12
N_HEADS_B = 4
DK_B = 32
DV_B = 64
GATE_RANK = 16
GATE_NORM = 16.0
GLA_CHUNK = 16
DILATED_BRANCHES = ((128, 1), (512, 4), (2048, 16))
WIN_MAX = 2048
Q_BLOCK = 128
N_BUCKETS = 32
BUCKET_MAX_DIST = 2048
EPS = 1e-6
WIDTH_A = N_HEADS_A * HEAD_DIM
WIDTH_BK = N_HEADS_B * DK_B
WIDTH_BV = N_HEADS_B * DV_B

LANES = 128
VMEM_LIMIT = 56 * 1024 * 1024
NEG_INF = float("-inf")


def _cparams(sem):
    return pltpu.CompilerParams(dimension_semantics=sem, vmem_limit_bytes=VMEM_LIMIT)


def _const_spec(shape):
    nd = len(shape)
    return pl.BlockSpec(shape, lambda *_: (0,) * nd, pipeline_mode=pl.Buffered(1))


def _rms_rows(x, gain):
    return x * lax.rsqrt(jnp.mean(x * x, axis=-1, keepdims=True) + EPS) * gain


def _ffn_body(x_ref, g_ref, w1_ref, w3_ref, w2_ref, o_ref, act_ref, *, fc):
    x = x_ref[...]
    h = _rms_rows(x, g_ref[...]).astype(BF16)
    for c in range(act_ref.shape[1] // fc):
        sl = pl.ds(c * fc, fc)
        a = jnp.dot(h, w1_ref[:, sl], preferred_element_type=F32)
        b = jnp.dot(h, w3_ref[:, sl], preferred_element_type=F32)
        act_ref[:, sl] = (a * jax.nn.sigmoid(a) * b).astype(BF16)
    o_ref[...] = x + 0.5 * jnp.dot(act_ref[...], w2_ref[...], preferred_element_type=F32)


def _ffn(x, gain, w1, w3, w2, *, tm=512, fc=256):
    t, d = x.shape
    f = w1.shape[1]
    tm = min(tm, t)
    return pl.pallas_call(
        functools.partial(_ffn_body, fc=fc),
        grid=(t // tm,),
        in_specs=[pl.BlockSpec((tm, d), lambda i: (i, 0)),
                  _const_spec((1, d)), _const_spec((d, f)), _const_spec((d, f)), _const_spec((f, d))],
        out_specs=pl.BlockSpec((tm, d), lambda i: (i, 0)),
        out_shape=jax.ShapeDtypeStruct((t, d), F32),
        scratch_shapes=[pltpu.VMEM((tm, f), BF16)],
        compiler_params=_cparams(("arbitrary",)),
        name="ffn",
    )(x, gain, w1, w3, w2)


_OFF_Q, _OFF_K, _OFF_V = 0, WIDTH_A, 2 * WIDTH_A
_OFF_QB = 3 * WIDTH_A
_OFF_KB = _OFF_QB + WIDTH_BK
_OFF_VB = _OFF_KB + WIDTH_BK
_OFF_RB = _OFF_VB + WIDTH_BV
_OFF_GL = _OFF_RB + WIDTH_BV
PROJ_PAD = _OFF_GL + LANES


def _proj_body(y_ref, g_ref, w_ref, qg_ref, kg_ref, m64_ref, wgk_ref, bgk_ref,
               qa_ref, ka_ref, va_ref, kab_ref, vab_ref, qb_ref, kb_ref, vb_ref, rb_ref, gk_ref):
    h = _rms_rows(y_ref[...], g_ref[...]).astype(BF16)

    def cols(off, width):
        return jnp.dot(h, w_ref[:, pl.ds(off, width)], preferred_element_type=F32)

    m64 = m64_ref[...]
    for gi in range(WIDTH_A // LANES):
        sl = pl.ds(gi * LANES, LANES)
        q = cols(_OFF_Q + gi * LANES, LANES)
        ms = jnp.dot((q * q).astype(BF16), m64, preferred_element_type=F32)
        qa_ref[:, sl] = (q * lax.rsqrt(ms + EPS) * qg_ref[...] * (HEAD_DIM ** -0.5)).astype(BF16)
        k = cols(_OFF_K + gi * LANES, LANES)
        ms = jnp.dot((k * k).astype(BF16), m64, preferred_element_type=F32)
        kn = k * lax.rsqrt(ms + EPS) * kg_ref[...]
        ka_ref[:, sl] = kn
        kab_ref[:, sl] = kn.astype(BF16)
        v = cols(_OFF_V + gi * LANES, LANES)
        va_ref[:, sl] = v
        vab_ref[:, sl] = v.astype(BF16)
    qb_ref[...] = cols(_OFF_QB, WIDTH_BK) * (DK_B ** -0.5)
    kb_ref[...] = cols(_OFF_KB, WIDTH_BK)
    vb_ref[...] = cols(_OFF_VB, WIDTH_BV)
    rb_ref[...] = cols(_OFF_RB, WIDTH_BV)
    glr = cols(_OFF_GL, LANES).astype(BF16)
    xg = jnp.dot(glr, wgk_ref[...], preferred_element_type=F32) + bgk_ref[...]
    gk_ref[...] = (jnp.minimum(xg, 0.0) - jnp.log(1.0 + jnp.exp(-jnp.abs(xg)))) * (1.0 / GATE_NORM)


def _proj(y, gain, w_pad, qg, kg, m64, wgk, bgk, *, tm=512):
    t, d = y.shape
    tm = min(tm, t)
    row = lambda w: pl.BlockSpec((tm, w), lambda i: (i, 0))
    widths = (WIDTH_A, WIDTH_A, WIDTH_A, WIDTH_A, WIDTH_A, WIDTH_BK, WIDTH_BK, WIDTH_BV, WIDTH_BV, WIDTH_BK)
    dtypes = (BF16, F32, F32, BF16, BF16, F32, F32, F32, F32, F32)
    return pl.pallas_call(
        _proj_body,
        grid=(t // tm,),
        in_specs=[row(d), _const_spec((1, d)), _const_spec(w_pad.shape),
                  _const_spec((1, LANES)), _const_spec((1, LANES)), _const_spec((LANES, LANES)),
                  _const_spec((LANES, WIDTH_BK)), _const_spec((1, WIDTH_BK))],
        out_specs=[row(w) for w in widths],
        out_shape=[jax.ShapeDtypeStruct((t, w), dt) for w, dt in zip(widths, dtypes)],
        compiler_params=_cparams(("arbitrary",)),
        name="proj",
    )(y, gain, w_pad, qg, kg, m64, wgk, bgk)


def _bias_body(rb_ref, idx_ref, add_ref, o_ref):
    idx = idx_ref[...]
    add = add_ref[...]
    for h in range(N_HEADS_A):
        acc = jnp.zeros(idx.shape, F32)
        for b in range(N_BUCKETS):
            acc = jnp.where(idx == b, rb_ref[b, h], acc)
        o_ref[h] = acc + add


def _bias_table(rel_bias, idx, add):
    return pl.pallas_call(
        _bias_body,
        in_specs=[pl.BlockSpec(memory_space=pltpu.SMEM),
                  pl.BlockSpec(idx.shape, lambda: (0, 0)), pl.BlockSpec(idx.shape, lambda: (0, 0))],
        out_specs=pl.BlockSpec((N_HEADS_A,) + idx.shape, lambda: (0, 0, 0)),
        out_shape=jax.ShapeDtypeStruct((N_HEADS_A,) + idx.shape, F32),
        name="bias_table",
    )(rel_bias, idx, add)


def _bucket(dist):
    max_exact = N_BUCKETS // 2
    d = jnp.maximum(dist, 1).astype(F32)
    large = max_exact + (jnp.log(d / max_exact) / math.log(BUCKET_MAX_DIST / max_exact)
                         * (N_BUCKETS - max_exact)).astype(jnp.int32)
    large = jnp.minimum(large, N_BUCKETS - 1)
    return jnp.where(dist < max_exact, dist, large)


def _prompt_bias_index(dilation, nk, first):
    i = jnp.arange(Q_BLOCK, dtype=jnp.int32)[:, None]
    j = jnp.arange(Q_BLOCK + nk, dtype=jnp.int32)[None, :]
    step = i - j + nk
    valid = (step >= 0) & (step <= nk)
    if first:
        valid = valid & (j >= nk)
    idx = _bucket(jnp.clip(step, 0, nk) * dilation)
    return idx, jnp.where(valid, 0.0, NEG_INF).astype(F32)


def _sample_bias_index(delta):
    count = jnp.zeros(delta.shape, jnp.int32)
    for window, dil in DILATED_BRANCHES:
        count += ((delta >= 0) & (delta % dil == 0) & (delta <= window)).astype(jnp.int32)
    add = jnp.where(count > 0, jnp.log(jnp.maximum(count, 1).astype(F32)), NEG_INF)
    return _bucket(jnp.maximum(delta, 0)), add.astype(F32)


def _attn_stage_body(*refs, first, last, nqb):
    q_ref, k_ref, kp_ref, v_ref, vp_ref, bias_ref = refs[:6]
    refs = refs[6:]
    if not first:
        m_in, l_in, acc_in = refs[:3]
        refs = refs[3:]
    if last:
        o_ref, kcat, vcat = refs
    else:
        m_out, l_out, acc_out, kcat, vcat = refs

    nk = Q_BLOCK
    kcat[0:nk, :] = kp_ref[0]
    kcat[nk:, :] = k_ref[0]
    vcat[0:nk, :] = vp_ref[0]
    vcat[nk:, :] = v_ref[0]
    first_tile = (pl.program_id(2) == 0).astype(jnp.int32)

    lane = lax.broadcasted_iota(jnp.int32, (Q_BLOCK, LANES), 1)
    lo = lane < HEAD_DIM
    lane_row = lax.broadcasted_iota(jnp.int32, (1, LANES), 1)
    head_sel = (jnp.where(lane_row < HEAD_DIM, 1.0, 0.0).astype(BF16),
                jnp.where(lane_row < HEAD_DIM, 0.0, 1.0).astype(BF16))

    for qb in range(nqb):
        rows = pl.ds(qb * Q_BLOCK, Q_BLOCK)
        krows = pl.ds(qb * Q_BLOCK, Q_BLOCK + nk)
        if not first:
            m_tile = m_in[0, rows, :]
            l_tile = l_in[0, rows, :]
        m_new_tile = jnp.zeros((Q_BLOCK, LANES), F32)
        l_new_tile = jnp.zeros((Q_BLOCK, LANES), F32)
        for hp in range(N_HEADS_A // 2):
            ls = pl.ds(hp * LANES, LANES)
            qblk = q_ref[0, rows, ls]
            kblk = kcat[krows, ls]
            vblk = vcat[krows, ls]
            pvs, alphas, linvs = [], [], []
            for hh in range(2):
                h = 2 * hp + hh
                s = lax.dot_general(qblk * head_sel[hh], kblk, (((1,), (1,)), ((), ())),
                                    preferred_element_type=F32)
                s = s + bias_ref[first_tile if qb == 0 else 0, h]
                m_cur = jnp.max(s, axis=1, keepdims=True)
                if first:
                    m_new = m_cur
                else:
                    m_prev = jnp.sum(jnp.where(lane == h, m_tile, 0.0), axis=1, keepdims=True)
                    l_prev = jnp.sum(jnp.where(lane == h, l_tile, 0.0), axis=1, keepdims=True)
                    m_new = jnp.maximum(m_prev, m_cur)
                    alpha = jnp.exp(m_prev - m_new)
                    alphas.append(alpha)
                p = jnp.exp(s - m_new)
                l_new = jnp.sum(p, axis=1, keepdims=True)
                if not first:
                    l_new = l_new + alpha * l_prev
                pvs.append(jnp.dot(p.astype(BF16), vblk, preferred_element_type=F32))
                if last:
                    linvs.append(1.0 / l_new)
                else:
                    m_new_tile = jnp.where(lane == h, m_new, m_new_tile)
                    l_new_tile = jnp.where(lane == h, l_new, l_new_tile)
            acc = jnp.where(lo, pvs[0], pvs[1])
            if not first:
                acc = acc + acc_in[0, rows, ls] * jnp.where(lo, alphas[0], alphas[1])
            if last:
                o_ref[0, rows, ls] = (acc * jnp.where(lo, linvs[0], linvs[1])).astype(BF16)
            else:
                acc_out[0, rows, ls] = acc
        if not last:
            m_out[0, rows, :] = m_new_tile
            l_out[0, rows, :] = l_new_tile


def _attn_stage(q, k, v, bias, carry, *, batch, seq, dilation, last):
    first = carry is None
    ln = seq // dilation
    tq = min(512, ln)
    nqb = tq // Q_BLOCK
    view = lambda a: a.reshape(batch, ln, dilation * a.shape[-1])
    blk = lambda w: pl.BlockSpec((1, tq, w), lambda b, r, t: (b, t, r))
    prev = pl.BlockSpec((1, Q_BLOCK, WIDTH_A), lambda b, r, t: (b, jnp.maximum(t * nqb - 1, 0), r))
    ins = [view(q), view(k), view(k), view(v), view(v), bias]
    in_specs = [blk(WIDTH_A), blk(WIDTH_A), prev, blk(WIDTH_A), prev, _const_spec(bias.shape)]
    if not first:
        ins += [view(c) for c in carry]
        in_specs += [blk(LANES), blk(LANES), blk(WIDTH_A)]
    if last:
        out_specs = blk(WIDTH_A)
        out_shape = jax.ShapeDtypeStruct((batch, ln, dilation * WIDTH_A), BF16)
    else:
        out_specs = [blk(LANES), blk(LANES), blk(WIDTH_A)]
        out_shape = [jax.ShapeDtypeStruct((batch, ln, dilation * w), F32) for w in (LANES, LANES, WIDTH_A)]
    out = pl.pallas_call(
        functools.partial(_attn_stage_body, first=first, last=last, nqb=nqb),
        grid=(batch, dilation, ln // tq),
        in_specs=in_specs, out_specs=out_specs, out_shape=out_shape,
        scratch_shapes=[pltpu.VMEM((tq + Q_BLOCK, WIDTH_A), BF16), pltpu.VMEM((tq + Q_BLOCK, WIDTH_A), BF16)],
        compiler_params=_cparams(("arbitrary", "arbitrary", "arbitrary")),
        name=f"attn_d{dilation}",
    )(*ins)
    if last:
        return out.reshape(batch * seq, WIDTH_A)
    return tuple(o.reshape(batch * seq, -1) for o in out)


def _attn_sample_body(q_ref, kn_ref, vn_ref, ck_ref, cv_ref, bc_ref, bn_ref,
                      o_ref, wk_ref, wv_ref, knp, vnp, *, t_new):
    w_buf = ck_ref.shape[1]
    wk_ref[0, 0:w_buf - t_new, :] = ck_ref[0, t_new:w_buf, :]
    wk_ref[0, w_buf - t_new:w_buf, :] = kn_ref[0]
    wv_ref[0, 0:w_buf - t_new, :] = cv_ref[0, t_new:w_buf, :]
    wv_ref[0, w_buf - t_new:w_buf, :] = vn_ref[0]
    knp[...] = jnp.zeros(knp.shape, BF16)
    vnp[...] = jnp.zeros(vnp.shape, BF16)
    knp[0:t_new, :] = kn_ref[0].astype(BF16)
    vnp[0:t_new, :] = vn_ref[0].astype(BF16)

    lane = lax.broadcasted_iota(jnp.int32, (t_new, LANES), 1)
    lo = lane < HEAD_DIM
    nt = (((1,), (1,)), ((), ()))
    for hp in range(N_HEADS_A // 2):
        ls = pl.ds(hp * LANES, LANES)
        q = q_ref[0, :, ls].astype(F32)
        zero = jnp.zeros_like(q)
        q2 = jnp.concatenate([jnp.where(lo, q, zero), jnp.where(lo, zero, q)], axis=0).astype(BF16)
        kc = ck_ref[0, :, ls].astype(BF16)
        vc = cv_ref[0, :, ls].astype(BF16)
        bias_c = jnp.concatenate([bc_ref[2 * hp], bc_ref[2 * hp + 1]], axis=0)
        bias_n = jnp.concatenate([bn_ref[2 * hp], bn_ref[2 * hp + 1]], axis=0)
        s_c = lax.dot_general(q2, kc, nt, preferred_element_type=F32) + bias_c
        s_n = lax.dot_general(q2, knp[:, ls], nt, preferred_element_type=F32) + bias_n
        m = jnp.maximum(jnp.max(s_c, axis=1, keepdims=True), jnp.max(s_n, axis=1, keepdims=True))
        p_c = jnp.exp(s_c - m)
        p_n = jnp.exp(s_n - m)
        l = jnp.sum(p_c, axis=1, keepdims=True) + jnp.sum(p_n, axis=1, keepdims=True)
        o = (jnp.dot(p_c.astype(BF16), vc, preferred_element_type=F32)
             + jnp.dot(p_n.astype(BF16), vnp[:, ls], preferred_element_type=F32)) * (1.0 / l)
        o_ref[0, :, ls] = jnp.where(lo, o[0:t_new], o[t_new:2 * t_new]).astype(BF16)


def _attn_sample(q, kn, vn, cache_k, cache_v, bias_c, bias_n):
    bd, t_new, _ = q.shape
    w_buf = cache_k.shape[1]
    new = pl.BlockSpec((1, t_new, WIDTH_A), lambda b: (b, 0, 0))
    win = pl.BlockSpec((1, w_buf, WIDTH_A), lambda b: (b, 0, 0))
    return pl.pallas_call(
        functools.partial(_attn_sample_body, t_new=t_new),
        grid=(bd,),
        in_specs=[new, new, new, win, win, _const_spec(bias_c.shape), _const_spec(bias_n.shape)],
        out_specs=[new, win, win],
        out_shape=[jax.ShapeDtypeStruct((bd, t_new, WIDTH_A), BF16),
                   jax.ShapeDtypeStruct(cache_k.shape, F32), jax.ShapeDtypeStruct(cache_v.shape, F32)],
        scratch_shapes=[pltpu.VMEM((LANES, WIDTH_A), BF16), pltpu.VMEM((LANES, WIDTH_A), BF16)],
        compiler_params=_cparams(("arbitrary",)),
        name="attn_sample",
    )(q, kn, vn, cache_k, cache_v, bias_c, bias_n)


def _gla_rows(q, k, g, v, chunk, get_state, put_state, o_ref):
    r = q.shape[0]
    ng = r // chunk
    row = lax.broadcasted_iota(jnp.int32, (r, WIDTH_BK), 0)
    pos = row % chunk
    b = g
    sh = 1
    while sh < chunk:
        b = b + jnp.where(pos >= sh, pltpu.roll(b, sh, axis=0), 0.0)
        sh *= 2
    b3 = b.reshape(ng, chunk, WIDTH_BK)
    q3 = q.reshape(ng, chunk, WIDTH_BK)
    k3 = k.reshape(ng, chunk, WIDTH_BK)
    v3 = v.reshape(ng, chunk, WIDTH_BV)
    bl3 = jnp.broadcast_to(b3[:, chunk - 1:chunk, :], b3.shape)
    bl = bl3.reshape(r, WIDTH_BK)

    gi = lax.broadcasted_iota(jnp.int32, (WIDTH_BK, WIDTH_BV), 0) // DK_B
    gj = lax.broadcasted_iota(jnp.int32, (WIDTH_BK, WIDTH_BV), 1) // DV_B
    expand = jnp.where(gi == gj, 1.0, 0.0).astype(BF16)
    di = lax.broadcasted_iota(jnp.int32, (WIDTH_BV, WIDTH_BK), 0) // DV_B
    dj = lax.broadcasted_iota(jnp.int32, (WIDTH_BV, WIDTH_BK), 1) // DK_B
    diag = di == dj

    s_idx = lax.broadcasted_iota(jnp.int32, (ng, chunk, WIDTH_BK), 1)
    t_idx = lax.broadcasted_iota(jnp.int32, (ng, chunk, WIDTH_BV), 1)
    o3 = jnp.zeros((ng, chunk, WIDTH_BV), F32)
    for t in range(chunk):
        bt = b3[:, t:t + 1, :]
        qt = q3[:, t:t + 1, :]
        dec = jnp.exp(jnp.where(s_idx <= t, bt - b3, NEG_INF))
        a = (dec * k3 * qt).reshape(r, WIDTH_BK).astype(BF16)
        w = jnp.dot(a, expand, preferred_element_type=F32).reshape(ng, chunk, WIDTH_BV)
        ot = jnp.sum(w * v3, axis=1, keepdims=True)
        o3 = jnp.where(t_idx == t, ot, o3)
    o_intra = o3.reshape(r, WIDTH_BV)

    qd = (q * jnp.exp(b)).astype(BF16)
    kd = (k * jnp.exp(bl - b)).astype(BF16)
    gdec = jnp.exp(bl)
    vt = v.T.astype(BF16)
    grp = row // chunk
    for j in range(ng):
        st = get_state(j)
        rows = slice(j * chunk, (j + 1) * chunk)
        o_inter = lax.dot_general(qd[rows], st.astype(BF16), (((1,), (1,)), ((), ())),
                                  preferred_element_type=F32)
        o_ref[rows, :] = o_intra[rows] + o_inter
        kj = jnp.where(grp == j, kd, jnp.zeros_like(kd))
        ds = jnp.dot(vt, kj, preferred_element_type=F32)
        put_state(j, st * gdec[j * chunk:j * chunk + 1, :] + jnp.where(diag, ds, 0.0))


def _gla_prompt_body(q_ref, k_ref, g_ref, v_ref, o_ref, s_ref, st_ref, *, chunk):
    @pl.when(pl.program_id(1) == 0)
    def _():
        st_ref[...] = jnp.zeros(st_ref.shape, F32)

    def put(j, s):
        st_ref[...] = s

    _gla_rows(q_ref[0], k_ref[0], g_ref[0], v_ref[0], chunk, lambda j: st_ref[...], put, o_ref.at[0])
    s_ref[0] = st_ref[...]


def _gla_prompt(q, k, g, v, *, batch, seq, rows=128):
    chunk = math.gcd(seq, GLA_CHUNK)
    rows = min(rows, seq)
    v3 = lambda a: a.reshape(batch, seq, a.shape[-1])
    blk = lambda w: pl.BlockSpec((1, rows, w), lambda b, i: (b, i, 0))
    o, st = pl.pallas_call(
        functools.partial(_gla_prompt_body, chunk=chunk),
        grid=(batch, seq // rows),
        in_specs=[blk(WIDTH_BK), blk(WIDTH_BK), blk(WIDTH_BK), blk(WIDTH_BV)],
        out_specs=[blk(WIDTH_BV), pl.BlockSpec((1, WIDTH_BV, WIDTH_BK), lambda b, i: (b, 0, 0))],
        out_shape=[jax.ShapeDtypeStruct((batch, seq, WIDTH_BV), F32),
                   jax.ShapeDtypeStruct((batch, WIDTH_BV, WIDTH_BK), F32)],
        scratch_shapes=[pltpu.VMEM((WIDTH_BV, WIDTH_BK), F32)],
        compiler_params=_cparams(("arbitrary", "arbitrary")),
        name="gla_prompt",
    )(v3(q), v3(k), v3(g), v3(v))
    return o.reshape(batch * seq, WIDTH_BV), st


def _gla_sample_body(q_ref, k_ref, g_ref, v_ref, s0_ref, o_ref, s1_ref, *, chunk):
    def put(j, s):
        s1_ref[j] = s

    _gla_rows(q_ref[...], k_ref[...], g_ref[...], v_ref[...], chunk, lambda j: s0_ref[j], put, o_ref)


def _gla_sample(q, k, g, v, st0, *, t_new, rows=128):
    t = q.shape[0]
    rows = min(rows, t)
    nb = rows // t_new
    blk = lambda w: pl.BlockSpec((rows, w), lambda i: (i, 0))
    sblk = pl.BlockSpec((nb, WIDTH_BV, WIDTH_BK), lambda i: (i, 0, 0))
    return pl.pallas_call(
        functools.partial(_gla_sample_body, chunk=t_new),
        grid=(t // rows,),
        in_specs=[blk(WIDTH_BK), blk(WIDTH_BK), blk(WIDTH_BK), blk(WIDTH_BV), sblk],
        out_specs=[blk(WIDTH_BV), sblk],
        out_shape=[jax.ShapeDtypeStruct((t, WIDTH_BV), F32), jax.ShapeDtypeStruct(st0.shape, F32)],
        compiler_params=_cparams(("arbitrary",)),
        name="gla_sample",
    )(q, k, g, v, st0)


def _state_to_blockdiag(s):
    b = s.shape[0]
    eye = jnp.eye(N_HEADS_B, dtype=s.dtype)
    return jnp.einsum('bhkv,hg->bhvgk', s, eye).reshape(b, WIDTH_BV, WIDTH_BK)


def _state_from_blockdiag(st):
    b = st.shape[0]
    s5 = st.reshape(b, N_HEADS_B, DV_B, N_HEADS_B, DK_B)
    idx = jnp.arange(N_HEADS_B)
    return s5[:, idx, :, idx, :].transpose(1, 0, 3, 2)


def _outproj_body(y_ref, oa_ref, ob_ref, rb_ref, gg_ref, m64_ref, wa_ref, wb_ref, o_ref):
    ob = ob_ref[...]
    ms = jnp.dot((ob * ob).astype(BF16), m64_ref[...], preferred_element_type=F32)
    rb = rb_ref[...]
    gated = ob * lax.rsqrt(ms + EPS) * gg_ref[...] * (rb * jax.nn.sigmoid(rb))
    o_ref[...] = (y_ref[...]
                  + jnp.dot(oa_ref[...], wa_ref[...], preferred_element_type=F32)
                  + jnp.dot(gated.astype(BF16), wb_ref[...], preferred_element_type=F32))


def _outproj(y, oa, ob, rb, gg, m64, wa, wb, *, tm=512):
    t, d = y.shape
    tm = min(tm, t)
    row = lambda w: pl.BlockSpec((tm, w), lambda i: (i, 0))
    return pl.pallas_call(
        _outproj_body,
        grid=(t // tm,),
        in_specs=[row(d), row(WIDTH_A), row(WIDTH_BV), row(WIDTH_BV), _const_spec((1, WIDTH_BV)),
                  _const_spec((WIDTH_BV, WIDTH_BV)), _const_spec((WIDTH_A, d)), _const_spec((WIDTH_BV, d))],
        out_specs=row(d),
        out_shape=jax.ShapeDtypeStruct((t, d), F32),
        compiler_params=_cparams(("arbitrary",)),
        name="outproj",
    )(y, oa, ob, rb, gg, m64, wa, wb)


def _head_mean_matrix(width):
    i = jnp.arange(width) // HEAD_DIM
    return jnp.where(i[:, None] == i[None, :], 1.0 / HEAD_DIM, 0.0).astype(BF16)


def kernel(x_prompt, x_sample, cache_win_k, cache_win_v, state_gla, ffn1_norm, ffn1_w1, ffn1_w3, ffn1_w2,
           mix_norm, w_in, q_norm, k_norm, rel_bias, w_gk2, b_gk, gla_norm, w_out, ffn2_norm, ffn2_w1,
           ffn2_w3, ffn2_w2):
    batch, seq, d_model = x_prompt.shape
    dec_batch, dec_seq, _ = x_sample.shape
    depth = ffn1_w1.shape[0]
    w_buf = cache_win_k.shape[2]
    assert seq % (Q_BLOCK * max(d for _, d in DILATED_BRANCHES)) == 0
    assert all(w // d == Q_BLOCK for w, d in DILATED_BRANCHES)
    assert GLA_CHUNK % dec_seq == 0 and LANES % dec_seq == 0

    m64_a = _head_mean_matrix(LANES)
    m64_b = _head_mean_matrix(WIDTH_BV)
    tile2 = lambda g: jnp.tile(g, LANES // HEAD_DIM)[None, :]

    stage_bias_idx = []
    for window, dil in DILATED_BRANCHES:
        nk = window // dil
        i0, a0 = _prompt_bias_index(dil, nk, first=False)
        i1, a1 = _prompt_bias_index(dil, nk, first=True)
        stage_bias_idx.append((jnp.concatenate([i0, i1], 0), jnp.concatenate([a0, a1], 0)))
    qi = jnp.arange(dec_seq, dtype=jnp.int32)[:, None]
    sc_idx, sc_add = _sample_bias_index(w_buf + qi - jnp.arange(w_buf, dtype=jnp.int32)[None, :])
    sn_idx, sn_add = _sample_bias_index(qi - jnp.arange(LANES, dtype=jnp.int32)[None, :])
    sn_add = jnp.where(jnp.arange(LANES)[None, :] < dec_seq, sn_add, NEG_INF)
    stage_bias = [_bias_table(rel_bias, i, a).reshape(N_HEADS_A, 2, Q_BLOCK, -1).transpose(1, 0, 2, 3)
                  for i, a in stage_bias_idx]
    bias_c = _bias_table(rel_bias, sc_idx, sc_add)
    bias_n = _bias_table(rel_bias, sn_idx, sn_add)

    yp = x_prompt.reshape(batch * seq, d_model)
    ys = x_sample.reshape(dec_batch * dec_seq, d_model)
    outs = [[] for _ in range(6)]
    for l in range(depth):
        bf = lambda w: w.astype(BF16)
        f1 = (ffn1_norm[l][None, :], bf(ffn1_w1[l]), bf(ffn1_w3[l]), bf(ffn1_w2[l]))
        f2 = (ffn2_norm[l][None, :], bf(ffn2_w1[l]), bf(ffn2_w3[l]), bf(ffn2_w2[l]))
        w_pad = jnp.pad(bf(w_in[l]), ((0, 0), (0, PROJ_PAD - w_in.shape[2])))
        wgk = jnp.pad(bf(w_gk2[l]), ((0, LANES - GATE_RANK), (0, 0)))
        pj = (mix_norm[l][None, :], w_pad, tile2(q_norm[l]), tile2(k_norm[l]), m64_a, wgk, b_gk[l][None, :])
        op = (jnp.tile(gla_norm[l], N_HEADS_B)[None, :], m64_b, bf(w_out[l][:WIDTH_A]), bf(w_out[l][WIDTH_A:]))

        y1 = _ffn(yp, *f1)
        qa, ka, va, kab, vab, qb, kb, vb, rb, gk = _proj(y1, *pj)
        carry = None
        stages = sorted(zip(DILATED_BRANCHES, stage_bias), key=lambda e: -e[0][1])
        for si, ((_, dil), bias) in enumerate(stages):
            carry = _attn_stage(qa, kab, vab, bias, carry, batch=batch, seq=seq, dilation=dil,
                                last=si == len(stages) - 1)
        ob, st = _gla_prompt(qb, kb, gk, vb, batch=batch, seq=seq)
        yp = _ffn(_outproj(y1, carry, ob, rb, *op), *f2)
        n_keep = min(WIN_MAX, seq)
        outs[0].append(ka.reshape(batch, seq, N_HEADS_A, HEAD_DIM)[:, seq - n_keep:])
        outs[1].append(va.reshape(batch, seq, N_HEADS_A, HEAD_DIM)[:, seq - n_keep:])
        outs[2].append(_state_from_blockdiag(st))

        y1 = _ffn(ys, *f1)
        qa, ka, va, _, _, qb, kb, vb, rb, gk = _proj(y1, *pj)
        r3 = lambda a: a.reshape(dec_batch, dec_seq, WIDTH_A)
        oa, wk, wv = _attn_sample(r3(qa), r3(ka), r3(va), cache_win_k[l].reshape(dec_batch, w_buf, WIDTH_A),
                                  cache_win_v[l].reshape(dec_batch, w_buf, WIDTH_A), bias_c, bias_n)
        ob, st = _gla_sample(qb, kb, gk, vb, _state_to_blockdiag(state_gla[l]), t_new=dec_seq)
        ys = _ffn(_outproj(y1, oa.reshape(dec_batch * dec_seq, WIDTH_A), ob, rb, *op), *f2)
        outs[3].append(wk.reshape(dec_batch, w_buf, N_HEADS_A, HEAD_DIM))
        outs[4].append(wv.reshape(dec_batch, w_buf, N_HEADS_A, HEAD_DIM))
        outs[5].append(_state_from_blockdiag(st))

    return (yp.reshape(batch, seq, d_model), ys.reshape(dec_batch, dec_seq, d_model),
            *(jnp.stack(o) for o in outs))
```

```python
import functools
import math

import jax
import jax.numpy as jnp
from jax import lax
from jax.experimental import pallas as pl
from jax.experimental.pallas import tpu as pltpu

F32 = jnp.float32
BF16 = jnp.bfloat16

HEAD_DIM = 64
N_HEADS_A = 12
N_HEADS_B = 4
DK_B = 32
DV_B = 64
GATE_RANK = 16
GATE_NORM = 16.0
GLA_CHUNK = 16
GLA_SUB = 128
DILATED_BRANCHES = ((128, 1), (512, 4), (2048, 16))
WIN_MAX = 2048
Q_BLOCK = 128
N_BUCKETS = 32
BUCKET_MAX_DIST = 2048
EPS = 1e-6
WIDTH_A = N_HEADS_A * HEAD_DIM
WIDTH_BK = N_HEADS_B * DK_B
WIDTH_BV = N_HEADS_B * DV_B

LANES = 128
MXU_N = 256
N_PAIRS = WIDTH_A // LANES
BLOCK_UNROLL = 16
VMEM_LIMIT = 56 * 1024 * 1024
NEG_INF = float("-inf")
NT_DIMS = (((1,), (1,)), ((), ()))


def _cparams(sem):
    return pltpu.CompilerParams(dimension_semantics=sem, vmem_limit_bytes=VMEM_LIMIT)


def _const_spec(shape):
    nd = len(shape)
    return pl.BlockSpec(shape, lambda *_: (0,) * nd, pipeline_mode=pl.Buffered(1))


def _rms_rows(x, gain):
    return x * lax.rsqrt(jnp.mean(x * x, axis=-1, keepdims=True) + EPS) * gain


def _ffn_body(x_ref, g_ref, w1_ref, w3_ref, w2_ref, o_ref, act_ref, *, fc):
    x = x_ref[...]
    h = _rms_rows(x, g_ref[...]).astype(BF16)
    for c in range(act_ref.shape[1] // fc):
        sl = pl.ds(c * fc, fc)
        a = jnp.dot(h, w1_ref[:, sl], preferred_element_type=F32)
        b = jnp.dot(h, w3_ref[:, sl], preferred_element_type=F32)
        act_ref[:, sl] = (a * jax.nn.sigmoid(a) * b).astype(BF16)
    o_ref[...] = x + 0.5 * jnp.dot(act_ref[...], w2_ref[...], preferred_element_type=F32)


def _ffn(x, gain, w1, w3, w2, *, tm=512, fc=256):
    t, d = x.shape
    f = w1.shape[1]
    tm = min(tm, t)
    return pl.pallas_call(
        functools.partial(_ffn_body, fc=fc),
        grid=(t // tm,),
        in_specs=[pl.BlockSpec((tm, d), lambda i: (i, 0)),
                  _const_spec((1, d)), _const_spec((d, f)), _const_spec((d, f)), _const_spec((f, d))],
        out_specs=pl.BlockSpec((tm, d), lambda i: (i, 0)),
        out_shape=jax.ShapeDtypeStruct((t, d), F32),
        scratch_shapes=[pltpu.VMEM((tm, f), BF16)],
        compiler_params=_cparams(("arbitrary",)),
        name="ffn",
    )(x, gain, w1, w3, w2)


_OFF_Q, _OFF_K, _OFF_V = 0, WIDTH_A, 2 * WIDTH_A
_OFF_QB = 3 * WIDTH_A
_OFF_KB = _OFF_QB + WIDTH_BK
_OFF_VB = _OFF_KB + WIDTH_BK
_OFF_RB = _OFF_VB + WIDTH_BV
_OFF_GL = _OFF_RB + WIDTH_BV
PROJ_PAD = _OFF_GL + MXU_N


def _proj_body(y_ref, g_ref, w_ref, qg_ref, kg_ref, m64_ref, wgk_ref, bgk_ref,
               qa_ref, ka_ref, va_ref, qb_ref, kb_ref, vb_ref, rb_ref, gk_ref):
    h = _rms_rows(y_ref[...], g_ref[...]).astype(BF16)

    def cols(off):
        return jnp.dot(h, w_ref[:, pl.ds(off, MXU_N)], preferred_element_type=F32)

    m64 = m64_ref[...]
    for gi in range(WIDTH_A // MXU_N):
        sl = pl.ds(gi * MXU_N, MXU_N)
        q = cols(_OFF_Q + gi * MXU_N)
        ms = jnp.dot((q * q).astype(BF16), m64, preferred_element_type=F32)
        qa_ref[:, sl] = q * lax.rsqrt(ms + EPS) * qg_ref[...] * (HEAD_DIM ** -0.5)
        k = cols(_OFF_K + gi * MXU_N)
        ms = jnp.dot((k * k).astype(BF16), m64, preferred_element_type=F32)
        ka_ref[:, sl] = k * lax.rsqrt(ms + EPS) * kg_ref[...]
        va_ref[:, sl] = cols(_OFF_V + gi * MXU_N)
    qkb = cols(_OFF_QB)
    qb_ref[...] = qkb[:, 0:WIDTH_BK] * (DK_B ** -0.5)
    kb_ref[...] = qkb[:, WIDTH_BK:2 * WIDTH_BK]
    vb_ref[...] = cols(_OFF_VB)
    rb_ref[...] = cols(_OFF_RB)
    glr = cols(_OFF_GL).astype(BF16)
    xg = jnp.dot(glr, wgk_ref[...], preferred_element_type=F32) + bgk_ref[...]
    gk_ref[...] = (jnp.minimum(xg, 0.0) - jnp.log(1.0 + jnp.exp(-jnp.abs(xg)))) * (1.0 / GATE_NORM)


def _proj(y, gain, w_pad, qg, kg, m64, wgk, bgk, *, tm=512):
    t, d = y.shape
    tm = min(tm, t)
    row = lambda w: pl.BlockSpec((tm, w), lambda i: (i, 0))
    widths = (WIDTH_A, WIDTH_A, WIDTH_A, WIDTH_BK, WIDTH_BK, WIDTH_BV, WIDTH_BV, WIDTH_BK)
    return pl.pallas_call(
        _proj_body,
        grid=(t // tm,),
        in_specs=[row(d), _const_spec((1, d)), _const_spec(w_pad.shape),
                  _const_spec((1, MXU_N)), _const_spec((1, MXU_N)), _const_spec((MXU_N, MXU_N)),
                  _const_spec((MXU_N, WIDTH_BK)), _const_spec((1, WIDTH_BK))],
        out_specs=[row(w) for w in widths],
        out_shape=[jax.ShapeDtypeStruct((t, w), F32) for w in widths],
        compiler_params=_cparams(("arbitrary",)),
        name="proj",
    )(y, gain, w_pad, qg, kg, m64, wgk, bgk)


def _bias_body(rb_ref, idx_ref, add_ref, o_ref):
    idx = idx_ref[...]
    add = add_ref[...]
    for h in range(N_HEADS_A):
        acc = jnp.zeros(idx.shape, F32)
        for b in range(N_BUCKETS):
            acc = jnp.where(idx == b, rb_ref[b, h], acc)
        o_ref[h] = acc + add


def _bias_table(rel_bias, idx, add):
    return pl.pallas_call(
        _bias_body,
        in_specs=[pl.BlockSpec(memory_space=pltpu.SMEM),
                  pl.BlockSpec(idx.shape, lambda: (0, 0)), pl.BlockSpec(idx.shape, lambda: (0, 0))],
        out_specs=pl.BlockSpec((N_HEADS_A,) + idx.shape, lambda: (0, 0, 0)),
        out_shape=jax.ShapeDtypeStruct((N_HEADS_A,) + idx.shape, F32),
        name="bias_table",
    )(rel_bias, idx, add)


def _bucket(dist):
    max_exact = N_BUCKETS // 2
    d = jnp.maximum(dist, 1).astype(F32)
    large = max_exact + (jnp.log(d / max_exact) / math.log(BUCKET_MAX_DIST / max_exact)
                         * (N_BUCKETS - max_exact)).astype(jnp.int32)
    large = jnp.minimum(large, N_BUCKETS - 1)
    return jnp.where(dist < max_exact, dist, large)


def _prompt_bias_index(dilation, nk, first):
    i = jnp.arange(Q_BLOCK, dtype=jnp.int32)[:, None]
    j = jnp.arange(Q_BLOCK + nk, dtype=jnp.int32)[None, :]
    step = i - j + nk
    valid = (step >= 0) & (step <= nk)
    if first:
        valid = valid & (j >= nk)
    idx = _bucket(jnp.clip(step, 0, nk) * dilation)
    return idx, jnp.where(valid, 0.0, NEG_INF).astype(F32)


def _sample_bias_index(delta):
    count = jnp.zeros(delta.shape, jnp.int32)
    for window, dil in DILATED_BRANCHES:
        count += ((delta >= 0) & (delta % dil == 0) & (delta <= window)).astype(jnp.int32)
    add = jnp.where(count > 0, jnp.log(jnp.maximum(count, 1).astype(F32)), NEG_INF)
    return _bucket(jnp.maximum(delta, 0)), add.astype(F32)


def _attn_prompt_body(q_ref, k_ref, v_ref, bias_ref, o_ref, qs, ks, vs, ms, ls, accs, *, dilations):
    seq = q_ref.shape[1]
    nblk = seq // Q_BLOCK
    lane = lax.broadcasted_iota(jnp.int32, (Q_BLOCK, LANES), 1)
    lo = lane < HEAD_DIM
    lane_row = lax.broadcasted_iota(jnp.int32, (1, LANES), 1)
    head_sel = (jnp.where(lane_row < HEAD_DIM, 1.0, 0.0).astype(BF16),
                jnp.where(lane_row < HEAD_DIM, 0.0, 1.0).astype(BF16))
    zeros_blk = jnp.zeros((Q_BLOCK, LANES), BF16)
    vs[:, LANES:] = jnp.ones((vs.shape[0], LANES), BF16)

    for bi, dil in enumerate(dilations):
        ln = seq // dil
        nqb = ln // Q_BLOCK
        kstride = ln + Q_BLOCK
        for r in range(dil):
            src = pl.ds(r, ln, stride=dil) if dil > 1 else pl.ds(0, ln)
            qs[pl.ds(r * ln, ln), :] = q_ref[0, src, :].astype(BF16)
            ks[pl.ds(r * kstride, Q_BLOCK), :] = zeros_blk
            vs[pl.ds(r * kstride, Q_BLOCK), 0:LANES] = zeros_blk
            ks[pl.ds(r * kstride + Q_BLOCK, ln), :] = k_ref[0, src, :].astype(BF16)
            vs[pl.ds(r * kstride + Q_BLOCK, ln), 0:LANES] = v_ref[0, src, :].astype(BF16)
        last = bi == len(dilations) - 1

        def block(ib, carry, bi=bi, dil=dil, ln=ln, nqb=nqb, kstride=kstride, last=last):
            r = ib // nqb
            qb = ib % nqb
            qrow = pl.multiple_of(r * ln + qb * Q_BLOCK, Q_BLOCK)
            krow = pl.multiple_of(r * kstride + qb * Q_BLOCK, Q_BLOCK)
            qblk = qs[pl.ds(qrow, Q_BLOCK), :]
            kblk = ks[pl.ds(krow, 2 * Q_BLOCK), :]
            vblk = vs[pl.ds(krow, 2 * Q_BLOCK), :]
            first_blk = (qb == 0).astype(jnp.int32)
            q2 = jnp.concatenate([qblk * head_sel[0], qblk * head_sel[1]], axis=0)
            s2 = lax.dot_general(q2, kblk, NT_DIMS, preferred_element_type=F32)
            ms_h, ps_h = [], []
            for hh in range(2):
                s = s2[hh * Q_BLOCK:(hh + 1) * Q_BLOCK] + bias_ref[bi, first_blk, hh]
                m = jnp.max(s, axis=1, keepdims=True)
                ms_h.append(m)
                ps_h.append(jnp.exp(s - m).astype(BF16))
            pv = jnp.dot(jnp.concatenate(ps_h, axis=0), vblk, preferred_element_type=F32)
            m_p = jnp.where(lo, ms_h[0], ms_h[1])
            l_p = jnp.where(lo, pv[0:Q_BLOCK, LANES:], pv[Q_BLOCK:2 * Q_BLOCK, LANES:])
            acc = jnp.where(lo, pv[0:Q_BLOCK, 0:LANES], pv[Q_BLOCK:2 * Q_BLOCK, 0:LANES])
            if not last:
                dst = pl.ds(qb * (Q_BLOCK * dil) + r, Q_BLOCK, stride=dil)
                ms[bi, dst, :] = m_p
                ls[bi, dst, :] = l_p
                accs[bi, dst, :] = acc
            else:
                rows = pl.ds(qrow, Q_BLOCK)
                m_all = m_p
                for bj in range(len(dilations) - 1):
                    m_all = jnp.maximum(m_all, ms[bj, rows, :])
                w = jnp.exp(m_p - m_all)
                l_all = w * l_p
                acc = w * acc
                for bj in range(len(dilations) - 1):
                    w = jnp.exp(ms[bj, rows, :] - m_all)
                    l_all = l_all + w * ls[bj, rows, :]
                    acc = acc + w * accs[bj, rows, :]
                o_ref[0, rows, :] = (acc / l_all).astype(BF16)
            return carry

        lax.fori_loop(0, nblk, block, 0, unroll=BLOCK_UNROLL)


def _attn_prompt(q, k, v, bias, *, batch, seq, dilations):
    assert dilations[-1] == 1
    dmax = max(dilations)
    v3 = lambda a: a.reshape(batch, seq, WIDTH_A)
    blk = pl.BlockSpec((1, seq, LANES), lambda b, hp: (b, 0, hp))
    nb = len(dilations)
    out = pl.pallas_call(
        functools.partial(_attn_prompt_body, dilations=dilations),
        grid=(batch, N_PAIRS),
        in_specs=[blk, blk, blk,
                  pl.BlockSpec((nb, 2, 2, Q_BLOCK, 2 * Q_BLOCK), lambda b, hp: (0, 0, hp, 0, 0))],
        out_specs=blk,
        out_shape=jax.ShapeDtypeStruct((batch, seq, WIDTH_A), BF16),
        scratch_shapes=[pltpu.VMEM((seq, LANES), BF16),
                        pltpu.VMEM((seq + dmax * Q_BLOCK, LANES), BF16),
                        pltpu.VMEM((seq + dmax * Q_BLOCK, 2 * LANES), BF16),
                        pltpu.VMEM((nb - 1, seq, LANES), F32),
                        pltpu.VMEM((nb - 1, seq, LANES), F32),
                        pltpu.VMEM((nb - 1, seq, LANES), F32)],
        compiler_params=_cparams(("arbitrary", "arbitrary")),
        name="attn_prompt",
    )(v3(q), v3(k), v3(v), bias)
    return out.reshape(batch * seq, WIDTH_A)


def _attn_sample_body(q_ref, kn_ref, vn_ref, ck_ref, cv_ref, bw_ref, bd_ref,
                      o_ref, wk_ref, a_ref, *, t_new):
    w_buf = ck_ref.shape[3]
    lane_q = lax.broadcasted_iota(jnp.int32, (t_new, LANES), 1)
    lo = lane_q < HEAD_DIM
    is_new = lax.broadcasted_iota(jnp.int32, (LANES, LANES), 1) >= LANES - t_new

    def shift(c_ref, n_ref, w_ref, hp):
        x = c_ref[0, hp]
        rolled = pltpu.roll(x, w_buf - t_new, axis=1)
        w_ref[0, hp, :, 0:w_buf - LANES] = rolled[:, 0:w_buf - LANES]
        w_ref[0, hp, :, w_buf - LANES:w_buf] = jnp.where(is_new, n_ref[0, hp], rolled[:, w_buf - LANES:w_buf])
        return w_ref[0, hp].astype(BF16), x[:, 0:LANES].astype(BF16)

    for hp in range(N_PAIRS):
        ls = pl.ds(hp * LANES, LANES)
        q = q_ref[0, :, ls]
        zero = jnp.zeros_like(q)
        q2 = jnp.concatenate([jnp.where(lo, q, zero), jnp.where(lo, zero, q)], axis=0).astype(BF16)
        kw, kd = shift(ck_ref, kn_ref, wk_ref, hp)
        vw, vd = shift(cv_ref, vn_ref, a_ref, hp)
        s_w = jnp.dot(q2, kw, preferred_element_type=F32) + bw_ref[hp]
        s_d = jnp.dot(q2, kd, preferred_element_type=F32) + bd_ref[hp]
        m = jnp.maximum(jnp.max(s_w, axis=1, keepdims=True), jnp.max(s_d, axis=1, keepdims=True))
        p_w = jnp.exp(s_w - m)
        p_d = jnp.exp(s_d - m)
        l = jnp.sum(p_w, axis=1, keepdims=True) + jnp.sum(p_d, axis=1, keepdims=True)
        o = (lax.dot_general(p_w.astype(BF16), vw, NT_DIMS, preferred_element_type=F32)
             + lax.dot_general(p_d.astype(BF16), vd, NT_DIMS, preferred_element_type=F32)) * (1.0 / l)
        o_ref[0, :, ls] = jnp.where(lo, o[0:t_new], o[t_new:2 * t_new]).astype(BF16)


def _attn_sample(q, kn, vn, cache_k, cache_v, bias_w, bias_d):
    bd, t_new, _ = q.shape
    w_buf = cache_k.shape[3]
    qs = pl.BlockSpec((1, t_new, WIDTH_A), lambda b: (b, 0, 0))
    new = pl.BlockSpec((1, N_PAIRS, LANES, LANES), lambda b: (b, 0, 0, 0))
    win = pl.BlockSpec((1, N_PAIRS, LANES, w_buf), lambda b: (b, 0, 0, 0))
    return pl.pallas_call(
        functools.partial(_attn_sample_body, t_new=t_new),
        grid=(bd,),
        in_specs=[qs, new, new, win, win, _const_spec(bias_w.shape), _const_spec(bias_d.shape)],
        out_specs=[qs, win, win],
        out_shape=[jax.ShapeDtypeStruct((bd, t_new, WIDTH_A), BF16),
                   jax.ShapeDtypeStruct(cache_k.shape, F32), jax.ShapeDtypeStruct(cache_v.shape, F32)],
        compiler_params=_cparams(("arbitrary",)),
        name="attn_sample",
    )(q, kn, vn, cache_k, cache_v, bias_w, bias_d)


def _gla_rows(q, k, g, v, chunk, get_state, put_state, o_ref, rsel_ref, a_ref):
    r = q.shape[0]
    ng = r // chunk
    row = lax.broadcasted_iota(jnp.int32, (r, WIDTH_BK), 0)
    pos = row % chunk
    b = g
    sh = 1
    while sh < chunk:
        b = b + jnp.where(pos >= sh, pltpu.roll(b, sh, axis=0), 0.0)
        sh *= 2
    b3 = b.reshape(ng, chunk, WIDTH_BK)
    q3 = q.reshape(ng, chunk, WIDTH_BK)
    k3 = k.reshape(ng, chunk, WIDTH_BK)
    bl3 = jnp.broadcast_to(b3[:, chunk - 1:chunk, :], b3.shape)
    bl = bl3.reshape(r, WIDTH_BK)

    gi = lax.broadcasted_iota(jnp.int32, (WIDTH_BK, WIDTH_BV), 0) // DK_B
    gj = lax.broadcasted_iota(jnp.int32, (WIDTH_BK, WIDTH_BV), 1) // DV_B
    expand = jnp.where(gi == gj, 1.0, 0.0).astype(BF16)
    di = lax.broadcasted_iota(jnp.int32, (WIDTH_BV, WIDTH_BK), 0) // DV_B
    dj = lax.broadcasted_iota(jnp.int32, (WIDTH_BV, WIDTH_BK), 1) // DK_B
    diag = di == dj

    nsub = GLA_SUB // chunk
    s_idx = lax.broadcasted_iota(jnp.int32, (nsub, chunk, WIDTH_BK), 1)
    o_intra = []
    for sb in range(r // GLA_SUB):
        gs = slice(sb * nsub, (sb + 1) * nsub)
        for t in range(chunk):
            dec = jnp.exp(jnp.where(s_idx <= t, b3[gs, t:t + 1, :] - b3[gs], NEG_INF))
            a = dec * k3[gs] * q3[gs, t:t + 1, :]
            a_ref[sb, pl.ds(t * GLA_SUB, GLA_SUB), :] = a.reshape(GLA_SUB, WIDTH_BK).astype(BF16)
        w = jnp.dot(a_ref[sb], expand, preferred_element_type=F32)
        wv = w.reshape(chunk, GLA_SUB, WIDTH_BV) * v[sb * GLA_SUB:(sb + 1) * GLA_SUB][None]
        o_intra.append(jnp.dot(rsel_ref[...], wv.reshape(chunk * GLA_SUB, WIDTH_BV).astype(BF16),
                               preferred_element_type=F32))

    qd = (q * jnp.exp(b)).astype(BF16)
    kd = (k * jnp.exp(bl - b)).astype(BF16)
    gdec = jnp.exp(bl)
    vt = v.T.astype(BF16)
    grp = lax.broadcasted_iota(jnp.int32, (GLA_SUB, WIDTH_BK), 0) // chunk
    for j in range(ng):
        st = get_state(j)
        rows = slice(j * chunk, (j + 1) * chunk)
        sb, jl = divmod(j, nsub)
        sub = slice(sb * GLA_SUB, (sb + 1) * GLA_SUB)
        o_inter = lax.dot_general(qd[rows], st.astype(BF16), NT_DIMS, preferred_element_type=F32)
        o_ref[rows, :] = o_intra[sb][jl * chunk:(jl + 1) * chunk] + o_inter
        kj = jnp.where(grp == jl, kd[sub], jnp.zeros((GLA_SUB, WIDTH_BK), BF16))
        ds = jnp.dot(vt[:, sub], kj, preferred_element_type=F32)
        put_state(j, st * gdec[j * chunk:j * chunk + 1, :] + jnp.where(diag, ds, 0.0))


def _gla_rsel(rows, chunk):
    n = jnp.arange(rows)[:, None]
    c = jnp.arange(chunk * rows)[None, :]
    return ((c // rows == n % chunk) & ((c % rows) // chunk == n // chunk)).astype(BF16)


def _gla_prompt_body(q_ref, k_ref, g_ref, v_ref, rsel_ref, o_ref, s_ref, st_ref, a_ref, *, chunk):
    @pl.when(pl.program_id(1) == 0)
    def _():
        st_ref[...] = jnp.zeros(st_ref.shape, F32)

    def put(j, s):
        st_ref[...] = s

    _gla_rows(q_ref[0], k_ref[0], g_ref[0], v_ref[0], chunk, lambda j: st_ref[...], put, o_ref.at[0],
              rsel_ref, a_ref)
    s_ref[0] = st_ref[...]


def _gla_prompt(q, k, g, v, *, batch, seq, rows=512):
    chunk = math.gcd(seq, GLA_CHUNK)
    rows = min(rows, seq)
    v3 = lambda a: a.reshape(batch, seq, a.shape[-1])
    blk = lambda w: pl.BlockSpec((1, rows, w), lambda b, i: (b, i, 0))
    o, st = pl.pallas_call(
        functools.partial(_gla_prompt_body, chunk=chunk),
        grid=(batch, seq // rows),
        in_specs=[blk(WIDTH_BK), blk(WIDTH_BK), blk(WIDTH_BK), blk(WIDTH_BV),
                  _const_spec((GLA_SUB, chunk * GLA_SUB))],
        out_specs=[blk(WIDTH_BV), pl.BlockSpec((1, WIDTH_BV, WIDTH_BK), lambda b, i: (b, 0, 0))],
        out_shape=[jax.ShapeDtypeStruct((batch, seq, WIDTH_BV), F32),
                   jax.ShapeDtypeStruct((batch, WIDTH_BV, WIDTH_BK), F32)],
        scratch_shapes=[pltpu.VMEM((WIDTH_BV, WIDTH_BK), F32), pltpu.VMEM((rows // GLA_SUB, chunk * GLA_SUB, WIDTH_BK), BF16)],
        compiler_params=_cparams(("arbitrary", "arbitrary")),
        name="gla_prompt",
    )(v3(q), v3(k), v3(g), v3(v), _gla_rsel(GLA_SUB, chunk))
    return o.reshape(batch * seq, WIDTH_BV), st


def _gla_sample_body(q_ref, k_ref, g_ref, v_ref, s0_ref, rsel_ref, o_ref, s1_ref, a_ref, *, chunk):
    def put(j, s):
        s1_ref[j] = s

    _gla_rows(q_ref[...], k_ref[...], g_ref[...], v_ref[...], chunk, lambda j: s0_ref[j], put, o_ref,
              rsel_ref, a_ref)


def _gla_sample(q, k, g, v, st0, *, t_new, rows=128):
    t = q.shape[0]
    rows = min(rows, t)
    nb = rows // t_new
    blk = lambda w: pl.BlockSpec((rows, w), lambda i: (i, 0))
    sblk = pl.BlockSpec((nb, WIDTH_BV, WIDTH_BK), lambda i: (i, 0, 0))
    return pl.pallas_call(
        functools.partial(_gla_sample_body, chunk=t_new),
        grid=(t // rows,),
        in_specs=[blk(WIDTH_BK), blk(WIDTH_BK), blk(WIDTH_BK), blk(WIDTH_BV), sblk,
                  _const_spec((GLA_SUB, t_new * GLA_SUB))],
        out_specs=[blk(WIDTH_BV), sblk],
        out_shape=[jax.ShapeDtypeStruct((t, WIDTH_BV), F32), jax.ShapeDtypeStruct(st0.shape, F32)],
        scratch_shapes=[pltpu.VMEM((rows // GLA_SUB, t_new * GLA_SUB, WIDTH_BK), BF16)],
        compiler_params=_cparams(("arbitrary",)),
        name="gla_sample",
    )(q, k, g, v, st0, _gla_rsel(GLA_SUB, t_new))


def _state_to_blockdiag(s):
    b = s.shape[0]
    eye = jnp.eye(N_HEADS_B, dtype=s.dtype)
    return jnp.einsum('bhkv,hg->bhvgk', s, eye).reshape(b, WIDTH_BV, WIDTH_BK)


def _state_from_blockdiag(st):
    b = st.shape[0]
    s5 = st.reshape(b, N_HEADS_B, DV_B, N_HEADS_B, DK_B)
    idx = jnp.arange(N_HEADS_B)
    return s5[:, idx, :, idx, :].transpose(1, 0, 3, 2)


def _outproj_body(y_ref, oa_ref, ob_ref, rb_ref, gg_ref, m64_ref, wa_ref, wb_ref, o_ref):
    ob = ob_ref[...]
    ms = jnp.dot((ob * ob).astype(BF16), m64_ref[...], preferred_element_type=F32)
    rb = rb_ref[...]
    gated = ob * lax.rsqrt(ms + EPS) * gg_ref[...] * (rb * jax.nn.sigmoid(rb))
    o_ref[...] = (y_ref[...]
                  + jnp.dot(oa_ref[...], wa_ref[...], preferred_element_type=F32)
                  + jnp.dot(gated.astype(BF16), wb_ref[...], preferred_element_type=F32))


def _outproj(y, oa, ob, rb, gg, m64, wa, wb, *, tm=512):
    t, d = y.shape
    tm = min(tm, t)
    row = lambda w: pl.BlockSpec((tm, w), lambda i: (i, 0))
    return pl.pallas_call(
        _outproj_body,
        grid=(t // tm,),
        in_specs=[row(d), row(WIDTH_A), row(WIDTH_BV), row(WIDTH_BV), _const_spec((1, WIDTH_BV)),
                  _const_spec((WIDTH_BV, WIDTH_BV)), _const_spec((WIDTH_A, d)), _const_spec((WIDTH_BV, d))],
        out_specs=row(d),
        out_shape=jax.ShapeDtypeStruct((t, d), F32),
        compiler_params=_cparams(("arbitrary",)),
        name="outproj",
    )(y, oa, ob, rb, gg, m64, wa, wb)


def _head_mean_matrix(width):
    i = jnp.arange(width) // HEAD_DIM
    return jnp.where(i[:, None] == i[None, :], 1.0 / HEAD_DIM, 0.0).astype(BF16)


def _to_lane_major(x, bd, t_new):
    xt = x.reshape(bd, t_new, N_PAIRS, LANES).transpose(0, 2, 3, 1)
    return jnp.pad(xt, ((0, 0), (0, 0), (0, 0), (LANES - t_new, 0)))


def kernel(x_prompt, x_sample, cache_win_k, cache_win_v, state_gla, ffn1_norm, ffn1_w1, ffn1_w3, ffn1_w2,
           mix_norm, w_in, q_norm, k_norm, rel_bias, w_gk2, b_gk, gla_norm, w_out, ffn2_norm, ffn2_w1,
           ffn2_w3, ffn2_w2):
    batch, seq, d_model = x_prompt.shape
    dec_batch, dec_seq, _ = x_sample.shape
    depth = ffn1_w1.shape[0]
    w_buf = cache_win_k.shape[2]
    dilations = tuple(sorted((d for _, d in DILATED_BRANCHES), reverse=True))
    assert seq % (Q_BLOCK * dilations[0]) == 0
    assert all(w // d == Q_BLOCK for w, d in DILATED_BRANCHES)
    assert GLA_CHUNK % dec_seq == 0 and LANES % dec_seq == 0

    assert WIDTH_BV == MXU_N and 2 * WIDTH_BK == MXU_N and WIDTH_A % MXU_N == 0
    m64_a = _head_mean_matrix(MXU_N)
    m64_b = _head_mean_matrix(WIDTH_BV)
    tile2 = lambda g: jnp.tile(g, MXU_N // HEAD_DIM)[None, :]

    stage_bias = []
    for dil in dilations:
        i0, a0 = _prompt_bias_index(dil, Q_BLOCK, first=False)
        i1, a1 = _prompt_bias_index(dil, Q_BLOCK, first=True)
        tbl = _bias_table(rel_bias, jnp.concatenate([i0, i1], 0), jnp.concatenate([a0, a1], 0))
        stage_bias.append(tbl.reshape(N_HEADS_A, 2, Q_BLOCK, 2 * Q_BLOCK).transpose(1, 0, 2, 3))
    stage_bias = jnp.stack(stage_bias)
    qi = jnp.arange(dec_seq, dtype=jnp.int32)[:, None]
    sw_idx, sw_add = _sample_bias_index(w_buf - dec_seq + qi - jnp.arange(w_buf, dtype=jnp.int32)[None, :])
    sd_idx, sd_add = _sample_bias_index(w_buf + qi - jnp.arange(LANES, dtype=jnp.int32)[None, :])
    sd_add = jnp.where(jnp.arange(LANES)[None, :] < dec_seq, sd_add, NEG_INF)
    pair_rows = lambda t: t.reshape(N_PAIRS, 2 * dec_seq, t.shape[-1])
    bias_w = pair_rows(_bias_table(rel_bias, sw_idx, sw_add))
    bias_d = pair_rows(_bias_table(rel_bias, sd_idx, sd_add))

    yp = x_prompt.reshape(batch * seq, d_model)
    ys = x_sample.reshape(dec_batch * dec_seq, d_model)
    outs = [[] for _ in range(6)]
    for l in range(depth):
        bf = lambda w: w.astype(BF16)
        f1 = (ffn1_norm[l][None, :], bf(ffn1_w1[l]), bf(ffn1_w3[l]), bf(ffn1_w2[l]))
        f2 = (ffn2_norm[l][None, :], bf(ffn2_w1[l]), bf(ffn2_w3[l]), bf(ffn2_w2[l]))
        w_pad = jnp.pad(bf(w_in[l]), ((0, 0), (0, PROJ_PAD - w_in.shape[2])))
        wgk = jnp.pad(bf(w_gk2[l]), ((0, MXU_N - GATE_RANK), (0, 0)))
        pj = (mix_norm[l][None, :], w_pad, tile2(q_norm[l]), tile2(k_norm[l]), m64_a, wgk, b_gk[l][None, :])
        op = (jnp.tile(gla_norm[l], N_HEADS_B)[None, :], m64_b, bf(w_out[l][:WIDTH_A]), bf(w_out[l][WIDTH_A:]))

        y1 = _ffn(yp, *f1)
        qa, ka, va, qb, kb, vb, rb, gk = _proj(y1, *pj)
        oa = _attn_prompt(qa, ka, va, stage_bias, batch=batch, seq=seq, dilations=dilations)
        ob, st = _gla_prompt(qb, kb, gk, vb, batch=batch, seq=seq)
        yp = _ffn(_outproj(y1, oa, ob, rb, *op), *f2)
        n_keep = min(WIN_MAX, seq)
        outs[0].append(ka.reshape(batch, seq, N_HEADS_A, HEAD_DIM)[:, seq - n_keep:])
        outs[1].append(va.reshape(batch, seq, N_HEADS_A, HEAD_DIM)[:, seq - n_keep:])
        outs[2].append(_state_from_blockdiag(st))

        y1 = _ffn(ys, *f1)
        qa, ka, va, qb, kb, vb, rb, gk = _proj(y1, *pj)
        lane_major = lambda c: c.transpose(0, 2, 3, 1).reshape(dec_batch, N_PAIRS, LANES, w_buf)
        oa, wk, wv = _attn_sample(qa.reshape(dec_batch, dec_seq, WIDTH_A),
                                  _to_lane_major(ka, dec_batch, dec_seq), _to_lane_major(va, dec_batch, dec_seq),
                                  lane_major(cache_win_k[l]), lane_major(cache_win_v[l]), bias_w, bias_d)
        ob, st = _gla_sample(qb, kb, gk, vb, _state_to_blockdiag(state_gla[l]), t_new=dec_seq)
        ys = _ffn(_outproj(y1, oa.reshape(dec_batch * dec_seq, WIDTH_A), ob, rb, *op), *f2)
        row_major = lambda w: w.reshape(dec_batch, N_HEADS_A, HEAD_DIM, w_buf).transpose(0, 3, 1, 2)
        outs[3].append(row_major(wk))
        outs[4].append(row_major(wv))
        outs[5].append(_state_from_blockdiag(st))

    return (yp.reshape(batch, seq, d_model), ys.reshape(dec_batch, dec_seq, d_model),
            *(jnp.stack(o) for o in outs))
```

```python
import functools
import itertools
import math

import jax
import jax.numpy as jnp
from jax import lax
from jax.experimental import pallas as pl
from jax.experimental.pallas import tpu as pltpu

F32 = jnp.float32
BF16 = jnp.bfloat16

HEAD_DIM = 64
N_HEADS_A = 12
N_HEADS_B = 4
DK_B = 32
DV_B = 64
GATE_RANK = 16
GATE_NORM = 16.0
GLA_CHUNK = 16
GLA_SUB = 128
DILATED_BRANCHES = ((128, 1), (512, 4), (2048, 16))
WIN_MAX = 2048
Q_BLOCK = 128
N_BUCKETS = 32
BUCKET_MAX_DIST = 2048
EPS = 1e-6
WIDTH_A = N_HEADS_A * HEAD_DIM
WIDTH_BK = N_HEADS_B * DK_B
WIDTH_BV = N_HEADS_B * DV_B

LANES = 128
MXU_N = 256
N_PAIRS = WIDTH_A // LANES
BLOCK_UNROLL = 16
VMEM_LIMIT = 56 * 1024 * 1024
NEG_INF = float("-inf")
NT_DIMS = (((1,), (1,)), ((), ()))


def _cparams(sem):
    return pltpu.CompilerParams(dimension_semantics=sem, vmem_limit_bytes=VMEM_LIMIT)


def _const_spec(shape):
    nd = len(shape)
    return pl.BlockSpec(shape, lambda *_: (0,) * nd, pipeline_mode=pl.Buffered(1))


def _rms_rows(x, gain):
    return x * lax.rsqrt(jnp.mean(x * x, axis=-1, keepdims=True) + EPS) * gain


def _ffn_tile(x_ref, g_ref, w1_ref, w3_ref, w2_ref, o_ref, act_ref, fc):
    x = x_ref[...]
    h = _rms_rows(x, g_ref[...]).astype(BF16)
    for c in range(act_ref.shape[1] // fc):
        sl = pl.ds(c * fc, fc)
        a = jnp.dot(h, w1_ref[:, sl], preferred_element_type=F32)
        b = jnp.dot(h, w3_ref[:, sl], preferred_element_type=F32)
        act_ref[:, sl] = (a * jax.nn.sigmoid(a) * b).astype(BF16)
        yield
    o_ref[...] = x + 0.5 * jnp.dot(act_ref[...], w2_ref[...], preferred_element_type=F32)
    yield


def _ffn_body(x_ref, g_ref, w1_ref, w3_ref, w2_ref, o_ref, act_ref, *, fc):
    for _ in _ffn_tile(x_ref, g_ref, w1_ref, w3_ref, w2_ref, o_ref, act_ref, fc):
        pass


def _ffn(x, gain, w1, w3, w2, *, tm=512, fc=256):
    t, d = x.shape
    f = w1.shape[1]
    tm = min(tm, t)
    return pl.pallas_call(
        functools.partial(_ffn_body, fc=fc),
        grid=(t // tm,),
        in_specs=[pl.BlockSpec((tm, d), lambda i: (i, 0)),
                  _const_spec((1, d)), _const_spec((d, f)), _const_spec((d, f)), _const_spec((f, d))],
        out_specs=pl.BlockSpec((tm, d), lambda i: (i, 0)),
        out_shape=jax.ShapeDtypeStruct((t, d), F32),
        scratch_shapes=[pltpu.VMEM((tm, f), BF16)],
        compiler_params=_cparams(("arbitrary",)),
        name="ffn",
    )(x, gain, w1, w3, w2)


def _ffn_sample_body(*refs, fc, t_new, n_prev):
    ffn_in, refs = refs[:5], refs[5:]
    smp_in, refs = refs[:7], refs[7 + n_prev:]
    o_ref, oa_ref, wk_ref, wv_ref, act_ref = refs
    for _ in itertools.zip_longest(_ffn_tile(*ffn_in, o_ref, act_ref, fc),
                                   _sample_attn_pairs(*smp_in, oa_ref, wk_ref, wv_ref, t_new=t_new,
                                                      seq_id=pl.program_id(0))):
        pass


def _ffn_sample(x, gain, w1, w3, w2, smp, *, group, n_groups, prev=None, fc=256):
    q, kn, vn, ck, cv, bw, bd_ = smp
    t, d = x.shape
    f = w1.shape[1]
    nseq, t_new, _ = q.shape
    w_buf = ck.shape[3]
    assert t % nseq == 0 and N_PAIRS % n_groups == 0
    tm = t // nseq
    assert tm % 8 == 0
    gp = N_PAIRS // n_groups
    qs = pl.BlockSpec((1, t_new, gp * LANES), lambda i: (i, 0, group))
    new = pl.BlockSpec((gp * LANES, LANES), lambda i: (group, i // (LANES // t_new)))
    win = pl.BlockSpec((1, gp, LANES, w_buf), lambda i: (i, group, 0, 0))
    bias = lambda a: pl.BlockSpec((gp,) + a.shape[1:], lambda i: (group, 0, 0), pipeline_mode=pl.Buffered(1))
    row = pl.BlockSpec((tm, d), lambda i: (i, 0))
    ins = [x, gain, w1, w3, w2, q, kn, vn, ck, cv, bw, bd_]
    in_specs = [row, _const_spec((1, d)), _const_spec((d, f)), _const_spec((d, f)), _const_spec((f, d)),
                qs, new, new, win, win, bias(bw), bias(bd_)]
    aliases = {}
    if prev is not None:
        aliases = {len(ins) + k: 1 + k for k in range(len(prev))}
        ins += list(prev)
        in_specs += [pl.BlockSpec(memory_space=pl.ANY)] * len(prev)
    return pl.pallas_call(
        functools.partial(_ffn_sample_body, fc=fc, t_new=t_new, n_prev=0 if prev is None else len(prev)),
        grid=(nseq,),
        in_specs=in_specs,
        out_specs=[row, qs, win, win],
        out_shape=[jax.ShapeDtypeStruct((t, d), F32), jax.ShapeDtypeStruct(q.shape, BF16),
                   jax.ShapeDtypeStruct(ck.shape, F32), jax.ShapeDtypeStruct(cv.shape, F32)],
        scratch_shapes=[pltpu.VMEM((tm, f), BF16)],
        input_output_aliases=aliases,
        compiler_params=_cparams(("arbitrary",)),
        name="ffn_sample",
    )(*ins)


_OFF_Q, _OFF_K, _OFF_V = 0, WIDTH_A, 2 * WIDTH_A
_OFF_QB = 3 * WIDTH_A
_OFF_KB = _OFF_QB + WIDTH_BK
_OFF_VB = _OFF_KB + WIDTH_BK
_OFF_RB = _OFF_VB + WIDTH_BV
_OFF_GL = _OFF_RB + WIDTH_BV
PROJ_PAD = _OFF_GL + MXU_N


def _proj_body(y_ref, g_ref, w_ref, qg_ref, kg_ref, m64_ref, wgk_ref, bgk_ref,
               qa_ref, ka_ref, va_ref, qb_ref, kb_ref, vb_ref, rb_ref, gk_ref, *kv_t_refs, keep):
    h = _rms_rows(y_ref[...], g_ref[...]).astype(BF16)
    if keep is not None:
        kt_ref, vt_ref = kv_t_refs
        in_keep = pl.program_id(0) % keep[0] >= keep[1]

    def cols(off):
        return jnp.dot(h, w_ref[:, pl.ds(off, MXU_N)], preferred_element_type=F32)

    m64 = m64_ref[...]
    for gi in range(WIDTH_A // MXU_N):
        sl = pl.ds(gi * MXU_N, MXU_N)
        q = cols(_OFF_Q + gi * MXU_N)
        ms = jnp.dot((q * q).astype(BF16), m64, preferred_element_type=F32)
        qa_ref[:, sl] = q * lax.rsqrt(ms + EPS) * qg_ref[...] * (HEAD_DIM ** -0.5)
        k = cols(_OFF_K + gi * MXU_N)
        ms = jnp.dot((k * k).astype(BF16), m64, preferred_element_type=F32)
        kn = k * lax.rsqrt(ms + EPS) * kg_ref[...]
        ka_ref[:, sl] = kn
        v = cols(_OFF_V + gi * MXU_N)
        va_ref[:, sl] = v
        if keep is not None:
            @pl.when(in_keep)
            def _(kn=kn, v=v, sl=sl):
                kt_ref[0, sl, :] = kn.T
                vt_ref[0, sl, :] = v.T
    qkb = cols(_OFF_QB)
    qb_ref[...] = qkb[:, 0:WIDTH_BK] * (DK_B ** -0.5)
    kb_ref[...] = qkb[:, WIDTH_BK:2 * WIDTH_BK]
    vb_ref[...] = cols(_OFF_VB)
    rb_ref[...] = cols(_OFF_RB)
    glr = cols(_OFF_GL).astype(BF16)
    xg = jnp.dot(glr, wgk_ref[...], preferred_element_type=F32) + bgk_ref[...]
    gk_ref[...] = (jnp.minimum(xg, 0.0) - jnp.log(1.0 + jnp.exp(-jnp.abs(xg)))) * (1.0 / GATE_NORM)


def _proj(y, gain, w_pad, qg, kg, m64, wgk, bgk, *, tm=512, seq=None, n_keep=None):
    t, d = y.shape
    tm = min(tm, t)
    row = lambda w: pl.BlockSpec((tm, w), lambda i: (i, 0))
    widths = (WIDTH_A, WIDTH_A, WIDTH_A, WIDTH_BK, WIDTH_BK, WIDTH_BV, WIDTH_BV, WIDTH_BK)
    out_specs = [row(w) for w in widths]
    out_shape = [jax.ShapeDtypeStruct((t, w), F32) for w in widths]
    keep = None
    if n_keep is not None:
        assert seq % tm == 0 and n_keep % tm == 0
        tps = seq // tm
        keep = (tps, (seq - n_keep) // tm)
        tail = pl.BlockSpec((1, WIDTH_A, tm), lambda i: (i // tps, 0, jnp.maximum(i % tps - keep[1], 0)))
        out_specs += [tail, tail]
        out_shape += [jax.ShapeDtypeStruct((t // seq, WIDTH_A, n_keep), F32)] * 2
    return pl.pallas_call(
        functools.partial(_proj_body, keep=keep),
        grid=(t // tm,),
        in_specs=[row(d), _const_spec((1, d)), _const_spec(w_pad.shape),
                  _const_spec((1, MXU_N)), _const_spec((1, MXU_N)), _const_spec((MXU_N, MXU_N)),
                  _const_spec((MXU_N, WIDTH_BK)), _const_spec((1, WIDTH_BK))],
        out_specs=out_specs,
        out_shape=out_shape,
        compiler_params=_cparams(("arbitrary",)),
        name="proj",
    )(y, gain, w_pad, qg, kg, m64, wgk, bgk)


def _bias_body(rb_ref, idx_ref, add_ref, o_ref):
    idx = idx_ref[...]
    add = add_ref[...]
    for h in range(N_HEADS_A):
        acc = jnp.zeros(idx.shape, F32)
        for b in range(N_BUCKETS):
            acc = jnp.where(idx == b, rb_ref[b, h], acc)
        o_ref[h] = acc + add


def _bias_table(rel_bias, idx, add):
    return pl.pallas_call(
        _bias_body,
        in_specs=[pl.BlockSpec(memory_space=pltpu.SMEM),
                  pl.BlockSpec(idx.shape, lambda: (0, 0)), pl.BlockSpec(idx.shape, lambda: (0, 0))],
        out_specs=pl.BlockSpec((N_HEADS_A,) + idx.shape, lambda: (0, 0, 0)),
        out_shape=jax.ShapeDtypeStruct((N_HEADS_A,) + idx.shape, F32),
        name="bias_table",
    )(rel_bias, idx, add)


def _bucket(dist):
    max_exact = N_BUCKETS // 2
    d = jnp.maximum(dist, 1).astype(F32)
    large = max_exact + (jnp.log(d / max_exact) / math.log(BUCKET_MAX_DIST / max_exact)
                         * (N_BUCKETS - max_exact)).astype(jnp.int32)
    large = jnp.minimum(large, N_BUCKETS - 1)
    return jnp.where(dist < max_exact, dist, large)


def _prompt_bias_index(dilation, nk, first):
    i = jnp.arange(Q_BLOCK, dtype=jnp.int32)[:, None]
    j = jnp.arange(Q_BLOCK + nk, dtype=jnp.int32)[None, :]
    step = i - j + nk
    valid = (step >= 0) & (step <= nk)
    if first:
        valid = valid & (j >= nk)
    idx = _bucket(jnp.clip(step, 0, nk) * dilation)
    return idx, jnp.where(valid, 0.0, NEG_INF).astype(F32)


def _sample_bias_index(delta):
    count = jnp.zeros(delta.shape, jnp.int32)
    for window, dil in DILATED_BRANCHES:
        count += ((delta >= 0) & (delta % dil == 0) & (delta <= window)).astype(jnp.int32)
    add = jnp.where(count > 0, jnp.log(jnp.maximum(count, 1).astype(F32)), NEG_INF)
    return _bucket(jnp.maximum(delta, 0)), add.astype(F32)


def _attn_prompt_body(q_ref, k_ref, v_ref, bias_ref, o_ref, qs, ks, vs, ms, ls, accs, *, dilations):
    seq = q_ref.shape[1]
    nblk = seq // Q_BLOCK
    lane = lax.broadcasted_iota(jnp.int32, (Q_BLOCK, LANES), 1)
    lo = lane < HEAD_DIM
    lane_row = lax.broadcasted_iota(jnp.int32, (1, LANES), 1)
    head_sel = (jnp.where(lane_row < HEAD_DIM, 1.0, 0.0).astype(BF16),
                jnp.where(lane_row < HEAD_DIM, 0.0, 1.0).astype(BF16))
    zeros_blk = jnp.zeros((Q_BLOCK, LANES), BF16)
    vs[:, LANES:] = jnp.ones((vs.shape[0], LANES), BF16)

    for bi, dil in enumerate(dilations):
        ln = seq // dil
        nqb = ln // Q_BLOCK
        kstride = ln + Q_BLOCK
        for r in range(dil):
            src = pl.ds(r, ln, stride=dil) if dil > 1 else pl.ds(0, ln)
            qs[pl.ds(r * ln, ln), :] = q_ref[0, src, :].astype(BF16)
            ks[pl.ds(r * kstride, Q_BLOCK), :] = zeros_blk
            vs[pl.ds(r * kstride, Q_BLOCK), 0:LANES] = zeros_blk
            ks[pl.ds(r * kstride + Q_BLOCK, ln), :] = k_ref[0, src, :].astype(BF16)
            vs[pl.ds(r * kstride + Q_BLOCK, ln), 0:LANES] = v_ref[0, src, :].astype(BF16)
        last = bi == len(dilations) - 1

        def block(ib, carry, bi=bi, dil=dil, ln=ln, nqb=nqb, kstride=kstride, last=last):
            r = ib // nqb
            qb = ib % nqb
            qrow = pl.multiple_of(r * ln + qb * Q_BLOCK, Q_BLOCK)
            krow = pl.multiple_of(r * kstride + qb * Q_BLOCK, Q_BLOCK)
            qblk = qs[pl.ds(qrow, Q_BLOCK), :]
            kblk = ks[pl.ds(krow, 2 * Q_BLOCK), :]
            vblk = vs[pl.ds(krow, 2 * Q_BLOCK), :]
            first_blk = jnp.asarray(qb == 0, jnp.int32)
            q2 = jnp.concatenate([qblk * head_sel[0], qblk * head_sel[1]], axis=0)
            s2 = lax.dot_general(q2, kblk, NT_DIMS, preferred_element_type=F32)
            ms_h, ps_h = [], []
            for hh in range(2):
                s = s2[hh * Q_BLOCK:(hh + 1) * Q_BLOCK] + bias_ref[bi, first_blk, hh]
                m = jnp.max(s, axis=1, keepdims=True)
                ms_h.append(m)
                ps_h.append(jnp.exp(s - m).astype(BF16))
            pv = jnp.dot(jnp.concatenate(ps_h, axis=0), vblk, preferred_element_type=F32)
            m_p = jnp.where(lo, ms_h[0], ms_h[1])
            l_p = jnp.where(lo, pv[0:Q_BLOCK, LANES:], pv[Q_BLOCK:2 * Q_BLOCK, LANES:])
            acc = jnp.where(lo, pv[0:Q_BLOCK, 0:LANES], pv[Q_BLOCK:2 * Q_BLOCK, 0:LANES])
            if not last:
                dst = pl.ds(qb * (Q_BLOCK * dil) + r, Q_BLOCK, stride=dil)
                ms[bi, dst, :] = m_p
                ls[bi, dst, :] = l_p
                accs[bi, dst, :] = acc
            else:
                rows = pl.ds(qrow, Q_BLOCK)
                m_all = m_p
                for bj in range(len(dilations) - 1):
                    m_all = jnp.maximum(m_all, ms[bj, rows, :])
                w = jnp.exp(m_p - m_all)
                l_all = w * l_p
                acc = w * acc
                for bj in range(len(dilations) - 1):
                    w = jnp.exp(ms[bj, rows, :] - m_all)
                    l_all = l_all + w * ls[bj, rows, :]
                    acc = acc + w * accs[bj, rows, :]
                o_ref[0, rows, :] = (acc / l_all).astype(BF16)
            return carry

        lax.fori_loop(0, nblk, block, 0, unroll=BLOCK_UNROLL)


def _attn_prompt(q, k, v, bias, *, batch, seq, dilations):
    assert dilations[-1] == 1
    dmax = max(dilations)
    v3 = lambda a: a.reshape(batch, seq, WIDTH_A)
    blk = pl.BlockSpec((1, seq, LANES), lambda b, hp: (b, 0, hp))
    nb = len(dilations)
    out = pl.pallas_call(
        functools.partial(_attn_prompt_body, dilations=dilations),
        grid=(batch, N_PAIRS),
        in_specs=[blk, blk, blk,
                  pl.BlockSpec((nb, 2, 2, Q_BLOCK, 2 * Q_BLOCK), lambda b, hp: (0, 0, hp, 0, 0))],
        out_specs=blk,
        out_shape=jax.ShapeDtypeStruct((batch, seq, WIDTH_A), BF16),
        scratch_shapes=[pltpu.VMEM((seq, LANES), BF16),
                        pltpu.VMEM((seq + dmax * Q_BLOCK, LANES), BF16),
                        pltpu.VMEM((seq + dmax * Q_BLOCK, 2 * LANES), BF16),
                        pltpu.VMEM((nb - 1, seq, LANES), F32),
                        pltpu.VMEM((nb - 1, seq, LANES), F32),
                        pltpu.VMEM((nb - 1, seq, LANES), F32)],
        compiler_params=_cparams(("arbitrary", "arbitrary")),
        name="attn_prompt",
    )(v3(q), v3(k), v3(v), bias)
    return out.reshape(batch * seq, WIDTH_A)


def _sample_attn_pairs(q_ref, kn_ref, vn_ref, ck_ref, cv_ref, bw_ref, bd_ref,
                       o_ref, wk_ref, wv_ref, *, t_new, seq_id):
    w_buf = ck_ref.shape[3]
    lane_q = lax.broadcasted_iota(jnp.int32, (t_new, LANES), 1)
    lo = lane_q < HEAD_DIM
    is_new = lax.broadcasted_iota(jnp.int32, (LANES, LANES), 1) >= LANES - t_new
    new_shift = (LANES - t_new) - t_new * (seq_id % (LANES // t_new))

    def shift(c_ref, n_ref, w_ref, hp):
        x = c_ref[0, hp]
        rolled = pltpu.roll(x, w_buf - t_new, axis=1)
        new = pltpu.roll(n_ref[pl.ds(hp * LANES, LANES), :], new_shift, axis=1)
        w_ref[0, hp, :, 0:w_buf - LANES] = rolled[:, 0:w_buf - LANES]
        w_ref[0, hp, :, w_buf - LANES:w_buf] = jnp.where(is_new, new, rolled[:, w_buf - LANES:w_buf])
        return w_ref[0, hp].astype(BF16), x[:, 0:LANES].astype(BF16)

    for hp in range(ck_ref.shape[1]):
        ls = pl.ds(hp * LANES, LANES)
        q = q_ref[0, :, ls]
        zero = jnp.zeros_like(q)
        q2 = jnp.concatenate([jnp.where(lo, q, zero), jnp.where(lo, zero, q)], axis=0).astype(BF16)
        kw, kd = shift(ck_ref, kn_ref, wk_ref, hp)
        yield
        vw, vd = shift(cv_ref, vn_ref, wv_ref, hp)
        yield
        s_w = jnp.dot(q2, kw, preferred_element_type=F32) + bw_ref[hp]
        s_d = jnp.dot(q2, kd, preferred_element_type=F32) + bd_ref[hp]
        yield
        m = jnp.maximum(jnp.max(s_w, axis=1, keepdims=True), jnp.max(s_d, axis=1, keepdims=True))
        p_w = jnp.exp(s_w - m)
        p_d = jnp.exp(s_d - m)
        l = jnp.sum(p_w, axis=1, keepdims=True) + jnp.sum(p_d, axis=1, keepdims=True)
        o = (lax.dot_general(p_w.astype(BF16), vw, NT_DIMS, preferred_element_type=F32)
             + lax.dot_general(p_d.astype(BF16), vd, NT_DIMS, preferred_element_type=F32)) * (1.0 / l)
        o_ref[0, :, ls] = jnp.where(lo, o[0:t_new], o[t_new:2 * t_new]).astype(BF16)
        yield


def _gla_rows(q, k, g, v, chunk, get_state, put_state, o_ref, rsel_ref, a_ref):
    r = q.shape[0]
    ng = r // chunk
    row = lax.broadcasted_iota(jnp.int32, (r, WIDTH_BK), 0)
    pos = row % chunk
    b = g
    sh = 1
    while sh < chunk:
        b = b + jnp.where(pos >= sh, pltpu.roll(b, sh, axis=0), 0.0)
        sh *= 2
    b3 = b.reshape(ng, chunk, WIDTH_BK)
    q3 = q.reshape(ng, chunk, WIDTH_BK)
    k3 = k.reshape(ng, chunk, WIDTH_BK)
    bl3 = jnp.broadcast_to(b3[:, chunk - 1:chunk, :], b3.shape)
    bl = bl3.reshape(r, WIDTH_BK)

    gi = lax.broadcasted_iota(jnp.int32, (WIDTH_BK, WIDTH_BV), 0) // DK_B
    gj = lax.broadcasted_iota(jnp.int32, (WIDTH_BK, WIDTH_BV), 1) // DV_B
    expand = jnp.where(gi == gj, 1.0, 0.0).astype(BF16)
    di = lax.broadcasted_iota(jnp.int32, (WIDTH_BV, WIDTH_BK), 0) // DV_B
    dj = lax.broadcasted_iota(jnp.int32, (WIDTH_BV, WIDTH_BK), 1) // DK_B
    diag = di == dj

    nsub = GLA_SUB // chunk
    s_idx = lax.broadcasted_iota(jnp.int32, (nsub, chunk, WIDTH_BK), 1)
    o_intra = []
    for sb in range(r // GLA_SUB):
        gs = slice(sb * nsub, (sb + 1) * nsub)
        for t in range(chunk):
            dec = jnp.exp(jnp.where(s_idx <= t, b3[gs, t:t + 1, :] - b3[gs], NEG_INF))
            a = dec * k3[gs] * q3[gs, t:t + 1, :]
            a_ref[sb, pl.ds(t * GLA_SUB, GLA_SUB), :] = a.reshape(GLA_SUB, WIDTH_BK).astype(BF16)
        w = jnp.dot(a_ref[sb], expand, preferred_element_type=F32)
        wv = w.reshape(chunk, GLA_SUB, WIDTH_BV) * v[sb * GLA_SUB:(sb + 1) * GLA_SUB][None]
        o_intra.append(jnp.dot(rsel_ref[...], wv.reshape(chunk * GLA_SUB, WIDTH_BV).astype(BF16),
                               preferred_element_type=F32))

    qd = (q * jnp.exp(b)).astype(BF16)
    kd = (k * jnp.exp(bl - b)).astype(BF16)
    gdec = jnp.exp(bl)
    vt = v.T.astype(BF16)
    grp = lax.broadcasted_iota(jnp.int32, (GLA_SUB, WIDTH_BK), 0) // chunk
    for j in range(ng):
        st = get_state(j)
        rows = slice(j * chunk, (j + 1) * chunk)
        sb, jl = divmod(j, nsub)
        sub = slice(sb * GLA_SUB, (sb + 1) * GLA_SUB)
        o_inter = lax.dot_general(qd[rows], st.astype(BF16), NT_DIMS, preferred_element_type=F32)
        o_ref[rows, :] = o_intra[sb][jl * chunk:(jl + 1) * chunk] + o_inter
        kj = jnp.where(grp == jl, kd[sub], jnp.zeros((GLA_SUB, WIDTH_BK), BF16))
        ds = jnp.dot(vt[:, sub], kj, preferred_element_type=F32)
        put_state(j, st * gdec[j * chunk:j * chunk + 1, :] + jnp.where(diag, ds, 0.0))


def _gla_rsel(rows, chunk):
    n = jnp.arange(rows)[:, None]
    c = jnp.arange(chunk * rows)[None, :]
    return ((c // rows == n % chunk) & ((c % rows) // chunk == n // chunk)).astype(BF16)


def _gla_prompt_body(q_ref, k_ref, g_ref, v_ref, rsel_ref, o_ref, s_ref, st_ref, a_ref, *, chunk):
    @pl.when(pl.program_id(1) == 0)
    def _():
        st_ref[...] = jnp.zeros(st_ref.shape, F32)

    def put(j, s):
        st_ref[...] = s

    _gla_rows(q_ref[0], k_ref[0], g_ref[0], v_ref[0], chunk, lambda j: st_ref[...], put, o_ref.at[0],
              rsel_ref, a_ref)
    s_ref[0] = st_ref[...]


def _gla_prompt(q, k, g, v, *, batch, seq, rows=512):
    chunk = math.gcd(seq, GLA_CHUNK)
    rows = min(rows, seq)
    v3 = lambda a: a.reshape(batch, seq, a.shape[-1])
    blk = lambda w: pl.BlockSpec((1, rows, w), lambda b, i: (b, i, 0))
    o, st = pl.pallas_call(
        functools.partial(_gla_prompt_body, chunk=chunk),
        grid=(batch, seq // rows),
        in_specs=[blk(WIDTH_BK), blk(WIDTH_BK), blk(WIDTH_BK), blk(WIDTH_BV),
                  _const_spec((GLA_SUB, chunk * GLA_SUB))],
        out_specs=[blk(WIDTH_BV), pl.BlockSpec((1, WIDTH_BV, WIDTH_BK), lambda b, i: (b, 0, 0))],
        out_shape=[jax.ShapeDtypeStruct((batch, seq, WIDTH_BV), F32),
                   jax.ShapeDtypeStruct((batch, WIDTH_BV, WIDTH_BK), F32)],
        scratch_shapes=[pltpu.VMEM((WIDTH_BV, WIDTH_BK), F32), pltpu.VMEM((rows // GLA_SUB, chunk * GLA_SUB, WIDTH_BK), BF16)],
        compiler_params=_cparams(("arbitrary", "arbitrary")),
        name="gla_prompt",
    )(v3(q), v3(k), v3(g), v3(v), _gla_rsel(GLA_SUB, chunk))
    return o.reshape(batch * seq, WIDTH_BV), st


def _gla_sample_body(q_ref, k_ref, g_ref, v_ref, s0_ref, rsel_ref, o_ref, s1_ref, a_ref, *, chunk):
    def put(j, s):
        s1_ref[j] = s

    _gla_rows(q_ref[...], k_ref[...], g_ref[...], v_ref[...], chunk, lambda j: s0_ref[j], put, o_ref,
              rsel_ref, a_ref)


def _gla_sample(q, k, g, v, st0, *, t_new, rows=128):
    t = q.shape[0]
    rows = min(rows, t)
    nb = rows // t_new
    blk = lambda w: pl.BlockSpec((rows, w), lambda i: (i, 0))
    sblk = pl.BlockSpec((nb, WIDTH_BV, WIDTH_BK), lambda i: (i, 0, 0))
    return pl.pallas_call(
        functools.partial(_gla_sample_body, chunk=t_new),
        grid=(t // rows,),
        in_specs=[blk(WIDTH_BK), blk(WIDTH_BK), blk(WIDTH_BK), blk(WIDTH_BV), sblk,
                  _const_spec((GLA_SUB, t_new * GLA_SUB))],
        out_specs=[blk(WIDTH_BV), sblk],
        out_shape=[jax.ShapeDtypeStruct((t, WIDTH_BV), F32), jax.ShapeDtypeStruct(st0.shape, F32)],
        scratch_shapes=[pltpu.VMEM((rows // GLA_SUB, t_new * GLA_SUB, WIDTH_BK), BF16)],
        compiler_params=_cparams(("arbitrary",)),
        name="gla_sample",
    )(q, k, g, v, st0, _gla_rsel(GLA_SUB, t_new))


def _state_to_blockdiag(s):
    b = s.shape[0]
    eye = jnp.eye(N_HEADS_B, dtype=s.dtype)
    return jnp.einsum('bhkv,hg->bhvgk', s, eye).reshape(b, WIDTH_BV, WIDTH_BK)


def _state_from_blockdiag(st):
    b = st.shape[0]
    s5 = st.reshape(b, N_HEADS_B, DV_B, N_HEADS_B, DK_B)
    idx = jnp.arange(N_HEADS_B)
    return s5[:, idx, :, idx, :].transpose(1, 0, 3, 2)


def _outproj_body(y_ref, oa_ref, ob_ref, rb_ref, gg_ref, m64_ref, wa_ref, wb_ref, o_ref):
    ob = ob_ref[...]
    ms = jnp.dot((ob * ob).astype(BF16), m64_ref[...], preferred_element_type=F32)
    rb = rb_ref[...]
    gated = ob * lax.rsqrt(ms + EPS) * gg_ref[...] * (rb * jax.nn.sigmoid(rb))
    o_ref[...] = (y_ref[...]
                  + jnp.dot(oa_ref[...], wa_ref[...], preferred_element_type=F32)
                  + jnp.dot(gated.astype(BF16), wb_ref[...], preferred_element_type=F32))


def _outproj(y, oa, ob, rb, gg, m64, wa, wb, *, tm=512):
    t, d = y.shape
    tm = min(tm, t)
    row = lambda w: pl.BlockSpec((tm, w), lambda i: (i, 0))
    return pl.pallas_call(
        _outproj_body,
        grid=(t // tm,),
        in_specs=[row(d), row(WIDTH_A), row(WIDTH_BV), row(WIDTH_BV), _const_spec((1, WIDTH_BV)),
                  _const_spec((WIDTH_BV, WIDTH_BV)), _const_spec((WIDTH_A, d)), _const_spec((WIDTH_BV, d))],
        out_specs=row(d),
        out_shape=jax.ShapeDtypeStruct((t, d), F32),
        compiler_params=_cparams(("arbitrary",)),
        name="outproj",
    )(y, oa, ob, rb, gg, m64, wa, wb)


def _head_mean_matrix(width):
    i = jnp.arange(width) // HEAD_DIM
    return jnp.where(i[:, None] == i[None, :], 1.0 / HEAD_DIM, 0.0).astype(BF16)


def kernel(x_prompt, x_sample, cache_win_k, cache_win_v, state_gla, ffn1_norm, ffn1_w1, ffn1_w3, ffn1_w2,
           mix_norm, w_in, q_norm, k_norm, rel_bias, w_gk2, b_gk, gla_norm, w_out, ffn2_norm, ffn2_w1,
           ffn2_w3, ffn2_w2):
    batch, seq, d_model = x_prompt.shape
    dec_batch, dec_seq, _ = x_sample.shape
    depth = ffn1_w1.shape[0]
    w_buf = cache_win_k.shape[2]
    dilations = tuple(sorted((d for _, d in DILATED_BRANCHES), reverse=True))
    assert seq % (Q_BLOCK * dilations[0]) == 0
    assert all(w // d == Q_BLOCK for w, d in DILATED_BRANCHES)
    assert GLA_CHUNK % dec_seq == 0 and LANES % dec_seq == 0

    assert WIDTH_BV == MXU_N and 2 * WIDTH_BK == MXU_N and WIDTH_A % MXU_N == 0
    m64_a = _head_mean_matrix(MXU_N)
    m64_b = _head_mean_matrix(WIDTH_BV)
    tile2 = lambda g: jnp.tile(g, MXU_N // HEAD_DIM)[None, :]

    stage_bias = []
    for dil in dilations:
        i0, a0 = _prompt_bias_index(dil, Q_BLOCK, first=False)
        i1, a1 = _prompt_bias_index(dil, Q_BLOCK, first=True)
        tbl = _bias_table(rel_bias, jnp.concatenate([i0, i1], 0), jnp.concatenate([a0, a1], 0))
        stage_bias.append(tbl.reshape(N_HEADS_A, 2, Q_BLOCK, 2 * Q_BLOCK).transpose(1, 0, 2, 3))
    stage_bias = jnp.stack(stage_bias)
    qi = jnp.arange(dec_seq, dtype=jnp.int32)[:, None]
    sw_idx, sw_add = _sample_bias_index(w_buf - dec_seq + qi - jnp.arange(w_buf, dtype=jnp.int32)[None, :])
    sd_idx, sd_add = _sample_bias_index(w_buf + qi - jnp.arange(LANES, dtype=jnp.int32)[None, :])
    sd_add = jnp.where(jnp.arange(LANES)[None, :] < dec_seq, sd_add, NEG_INF)
    pair_rows = lambda t: t.reshape(N_PAIRS, 2 * dec_seq, t.shape[-1])
    bias_w = pair_rows(_bias_table(rel_bias, sw_idx, sw_add))
    bias_d = pair_rows(_bias_table(rel_bias, sd_idx, sd_add))

    yp = x_prompt.reshape(batch * seq, d_model)
    ys = x_sample.reshape(dec_batch * dec_seq, d_model)
    outs = [[] for _ in range(6)]
    for l in range(depth):
        bf = lambda w: w.astype(BF16)
        f1 = (ffn1_norm[l][None, :], bf(ffn1_w1[l]), bf(ffn1_w3[l]), bf(ffn1_w2[l]))
        f2 = (ffn2_norm[l][None, :], bf(ffn2_w1[l]), bf(ffn2_w3[l]), bf(ffn2_w2[l]))
        w_pad = jnp.pad(bf(w_in[l]), ((0, 0), (0, PROJ_PAD - w_in.shape[2])))
        wgk = jnp.pad(bf(w_gk2[l]), ((0, MXU_N - GATE_RANK), (0, 0)))
        pj = (mix_norm[l][None, :], w_pad, tile2(q_norm[l]), tile2(k_norm[l]), m64_a, wgk, b_gk[l][None, :])
        op = (jnp.tile(gla_norm[l], N_HEADS_B)[None, :], m64_b, bf(w_out[l][:WIDTH_A]), bf(w_out[l][WIDTH_A:]))

        ys1 = _ffn(ys, *f1)
        n_new = dec_batch * dec_seq
        qa_s, _, _, qb_s, kb_s, vb_s, rb_s, gk_s, kt_s, vt_s = _proj(ys1, *pj, seq=n_new, n_keep=n_new)
        lane_major = lambda c: c.transpose(0, 2, 3, 1).reshape(dec_batch, N_PAIRS, LANES, w_buf)
        smp = (qa_s.reshape(dec_batch, dec_seq, WIDTH_A), kt_s[0], vt_s[0],
               lane_major(cache_win_k[l]), lane_major(cache_win_v[l]), bias_w, bias_d)

        y1, *smp_out = _ffn_sample(yp, *f1, smp, group=0, n_groups=2)
        n_keep = min(WIN_MAX, seq)
        qa, ka, va, qb, kb, vb, rb, gk, kt, vt = _proj(y1, *pj, seq=seq, n_keep=n_keep)
        oa = _attn_prompt(qa, ka, va, stage_bias, batch=batch, seq=seq, dilations=dilations)
        ob, st = _gla_prompt(qb, kb, gk, vb, batch=batch, seq=seq)
        yp, oa_s, wk, wv = _ffn_sample(_outproj(y1, oa, ob, rb, *op), *f2, smp, group=1, n_groups=2,
                                       prev=smp_out)
        keep_major = lambda a: a.reshape(batch, N_HEADS_A, HEAD_DIM, n_keep).transpose(0, 3, 1, 2)
        outs[0].append(keep_major(kt))
        outs[1].append(keep_major(vt))
        outs[2].append(_state_from_blockdiag(st))

        ob, st = _gla_sample(qb_s, kb_s, gk_s, vb_s, _state_to_blockdiag(state_gla[l]), t_new=dec_seq)
        ys = _ffn(_outproj(ys1, oa_s.reshape(dec_batch * dec_seq, WIDTH_A), ob, rb_s, *op), *f2)
        row_major = lambda w: w.reshape(dec_batch, N_HEADS_A, HEAD_DIM, w_buf).transpose(0, 3, 1, 2)
        outs[3].append(row_major(wk))
        outs[4].append(row_major(wv))
        outs[5].append(_state_from_blockdiag(st))

    return (yp.reshape(batch, seq, d_model), ys.reshape(dec_batch, dec_seq, d_model),
            *(jnp.stack(o) for o in outs))
```

```python
import functools
import itertools
import math

import jax
import jax.numpy as jnp
from jax import lax
from jax.experimental import pallas as pl
from jax.experimental.pallas import tpu as pltpu

F32 = jnp.float32
BF16 = jnp.bfloat16

HEAD_DIM = 64
N_HEADS_A = 12
N_HEADS_B = 4
DK_B = 32
DV_B = 64
GATE_RANK = 16
GATE_NORM = 16.0
GLA_CHUNK = 16
GLA_SUB = 128
DILATED_BRANCHES = ((128, 1), (512, 4), (2048, 16))
WIN_MAX = 2048
Q_BLOCK = 128
N_BUCKETS = 32
BUCKET_MAX_DIST = 2048
EPS = 1e-6
WIDTH_A = N_HEADS_A * HEAD_DIM
WIDTH_BK = N_HEADS_B * DK_B
WIDTH_BV = N_HEADS_B * DV_B

LANES = 128
MXU_N = 256
N_PAIRS = WIDTH_A // LANES
BLOCK_UNROLL = 16
VMEM_LIMIT = 56 * 1024 * 1024
NEG_INF = float("-inf")
NT_DIMS = (((1,), (1,)), ((), ()))


def _cparams(sem):
    return pltpu.CompilerParams(dimension_semantics=sem, vmem_limit_bytes=VMEM_LIMIT)


def _const_spec(shape):
    nd = len(shape)
    return pl.BlockSpec(shape, lambda *_: (0,) * nd, pipeline_mode=pl.Buffered(1))


def _rms_rows(x, gain):
    return x * lax.rsqrt(jnp.mean(x * x, axis=-1, keepdims=True) + EPS) * gain


def _ffn_tile(x_ref, g_ref, w1_ref, w3_ref, w2_ref, o_ref, act_ref, fc):
    x = x_ref[...]
    h = _rms_rows(x, g_ref[...]).astype(BF16)
    for c in range(act_ref.shape[1] // fc):
        sl = pl.ds(c * fc, fc)
        a = jnp.dot(h, w1_ref[:, sl], preferred_element_type=F32)
        b = jnp.dot(h, w3_ref[:, sl], preferred_element_type=F32)
        act_ref[:, sl] = (a * jax.nn.sigmoid(a) * b).astype(BF16)
        yield
    o_ref[...] = x + 0.5 * jnp.dot(act_ref[...], w2_ref[...], preferred_element_type=F32)
    yield


def _ffn_body(x_ref, g_ref, w1_ref, w3_ref, w2_ref, o_ref, act_ref, *, fc):
    for _ in _ffn_tile(x_ref, g_ref, w1_ref, w3_ref, w2_ref, o_ref, act_ref, fc):
        pass


def _ffn(x, gain, w1, w3, w2, *, tm=512, fc=256):
    t, d = x.shape
    f = w1.shape[1]
    tm = min(tm, t)
    return pl.pallas_call(
        functools.partial(_ffn_body, fc=fc),
        grid=(t // tm,),
        in_specs=[pl.BlockSpec((tm, d), lambda i: (i, 0)),
                  _const_spec((1, d)), _const_spec((d, f)), _const_spec((d, f)), _const_spec((f, d))],
        out_specs=pl.BlockSpec((tm, d), lambda i: (i, 0)),
        out_shape=jax.ShapeDtypeStruct((t, d), F32),
        scratch_shapes=[pltpu.VMEM((tm, f), BF16)],
        compiler_params=_cparams(("arbitrary",)),
        name="ffn",
    )(x, gain, w1, w3, w2)


def _ffn_side_body(*refs, fc, side, n_side_in, n_side_out):
    ffn_in, refs = refs[:5], refs[5:]
    side_in, refs = refs[:n_side_in], refs[n_side_in:]
    o_ref, side_out, act_ref = refs[0], refs[1:1 + n_side_out], refs[1 + n_side_out]
    for _ in itertools.zip_longest(_ffn_tile(*ffn_in, o_ref, act_ref, fc),
                                   side(*side_in, *side_out, seq_id=pl.program_id(0))):
        pass


def _ffn_side(x, gain, w1, w3, w2, side, side_ins, side_in_specs, side_out_specs, side_out_shape,
              *, steps, fc=256):
    t, d = x.shape
    f = w1.shape[1]
    assert t % steps == 0
    tm = t // steps
    assert tm % 8 == 0
    row = pl.BlockSpec((tm, d), lambda i: (i, 0))
    return pl.pallas_call(
        functools.partial(_ffn_side_body, fc=fc, side=side, n_side_in=len(side_ins),
                          n_side_out=len(side_out_specs)),
        grid=(steps,),
        in_specs=[row, _const_spec((1, d)), _const_spec((d, f)), _const_spec((d, f)), _const_spec((f, d)),
                  *side_in_specs],
        out_specs=[row, *side_out_specs],
        out_shape=[jax.ShapeDtypeStruct((t, d), F32), *side_out_shape],
        scratch_shapes=[pltpu.VMEM((tm, f), BF16)],
        compiler_params=_cparams(("arbitrary",)),
        name="ffn_side",
    )(x, gain, w1, w3, w2, *side_ins)


def _sample_specs(t_new, w_buf):
    per_tile = LANES // t_new
    return dict(
        q=pl.BlockSpec((1, t_new, WIDTH_A), lambda i: (i, 0, 0)),
        new=pl.BlockSpec((WIDTH_A, LANES), lambda i: (0, i // per_tile)),
        win=pl.BlockSpec((1, N_PAIRS, LANES, w_buf), lambda i: (i, 0, 0, 0)),
        pw=pl.BlockSpec((1, N_PAIRS, 2 * t_new, w_buf), lambda i: (i, 0, 0, 0)),
        pd=pl.BlockSpec((1, N_PAIRS, 2 * t_new, LANES), lambda i: (i, 0, 0, 0)),
    )


_OFF_Q, _OFF_K, _OFF_V = 0, WIDTH_A, 2 * WIDTH_A
_OFF_QB = 3 * WIDTH_A
_OFF_KB = _OFF_QB + WIDTH_BK
_OFF_VB = _OFF_KB + WIDTH_BK
_OFF_RB = _OFF_VB + WIDTH_BV
_OFF_GL = _OFF_RB + WIDTH_BV
PROJ_PAD = _OFF_GL + MXU_N


def _proj_body(y_ref, g_ref, w_ref, qg_ref, kg_ref, m64_ref, wgk_ref, bgk_ref,
               qa_ref, ka_ref, va_ref, qb_ref, kb_ref, vb_ref, rb_ref, gk_ref, *kv_t_refs, keep):
    h = _rms_rows(y_ref[...], g_ref[...]).astype(BF16)
    if keep is not None:
        kt_ref, vt_ref = kv_t_refs
        in_keep = pl.program_id(0) % keep[0] >= keep[1]

    def cols(off):
        return jnp.dot(h, w_ref[:, pl.ds(off, MXU_N)], preferred_element_type=F32)

    m64 = m64_ref[...]
    for gi in range(WIDTH_A // MXU_N):
        sl = pl.ds(gi * MXU_N, MXU_N)
        q = cols(_OFF_Q + gi * MXU_N)
        ms = jnp.dot((q * q).astype(BF16), m64, preferred_element_type=F32)
        qa_ref[:, sl] = q * lax.rsqrt(ms + EPS) * qg_ref[...] * (HEAD_DIM ** -0.5)
        k = cols(_OFF_K + gi * MXU_N)
        ms = jnp.dot((k * k).astype(BF16), m64, preferred_element_type=F32)
        kn = k * lax.rsqrt(ms + EPS) * kg_ref[...]
        ka_ref[:, sl] = kn
        v = cols(_OFF_V + gi * MXU_N)
        va_ref[:, sl] = v
        if keep is not None:
            @pl.when(in_keep)
            def _(kn=kn, v=v, sl=sl):
                kt_ref[0, sl, :] = kn.T
                vt_ref[0, sl, :] = v.T
    qkb = cols(_OFF_QB)
    qb_ref[...] = qkb[:, 0:WIDTH_BK] * (DK_B ** -0.5)
    kb_ref[...] = qkb[:, WIDTH_BK:2 * WIDTH_BK]
    vb_ref[...] = cols(_OFF_VB)
    rb_ref[...] = cols(_OFF_RB)
    glr = cols(_OFF_GL).astype(BF16)
    xg = jnp.dot(glr, wgk_ref[...], preferred_element_type=F32) + bgk_ref[...]
    gk_ref[...] = (jnp.minimum(xg, 0.0) - jnp.log(1.0 + jnp.exp(-jnp.abs(xg)))) * (1.0 / GATE_NORM)


def _proj(y, gain, w_pad, qg, kg, m64, wgk, bgk, *, tm=512, seq=None, n_keep=None):
    t, d = y.shape
    tm = min(tm, t)
    row = lambda w: pl.BlockSpec((tm, w), lambda i: (i, 0))
    widths = (WIDTH_A, WIDTH_A, WIDTH_A, WIDTH_BK, WIDTH_BK, WIDTH_BV, WIDTH_BV, WIDTH_BK)
    out_specs = [row(w) for w in widths]
    out_shape = [jax.ShapeDtypeStruct((t, w), F32) for w in widths]
    keep = None
    if n_keep is not None:
        assert seq % tm == 0 and n_keep % tm == 0
        tps = seq // tm
        keep = (tps, (seq - n_keep) // tm)
        tail = pl.BlockSpec((1, WIDTH_A, tm), lambda i: (i // tps, 0, jnp.maximum(i % tps - keep[1], 0)))
        out_specs += [tail, tail]
        out_shape += [jax.ShapeDtypeStruct((t // seq, WIDTH_A, n_keep), F32)] * 2
    return pl.pallas_call(
        functools.partial(_proj_body, keep=keep),
        grid=(t // tm,),
        in_specs=[row(d), _const_spec((1, d)), _const_spec(w_pad.shape),
                  _const_spec((1, MXU_N)), _const_spec((1, MXU_N)), _const_spec((MXU_N, MXU_N)),
                  _const_spec((MXU_N, WIDTH_BK)), _const_spec((1, WIDTH_BK))],
        out_specs=out_specs,
        out_shape=out_shape,
        compiler_params=_cparams(("arbitrary",)),
        name="proj",
    )(y, gain, w_pad, qg, kg, m64, wgk, bgk)


def _bias_body(rb_ref, idx_ref, add_ref, o_ref):
    idx = idx_ref[...]
    add = add_ref[...]
    for h in range(N_HEADS_A):
        acc = jnp.zeros(idx.shape, F32)
        for b in range(N_BUCKETS):
            acc = jnp.where(idx == b, rb_ref[b, h], acc)
        o_ref[h] = acc + add


def _bias_table(rel_bias, idx, add):
    return pl.pallas_call(
        _bias_body,
        in_specs=[pl.BlockSpec(memory_space=pltpu.SMEM),
                  pl.BlockSpec(idx.shape, lambda: (0, 0)), pl.BlockSpec(idx.shape, lambda: (0, 0))],
        out_specs=pl.BlockSpec((N_HEADS_A,) + idx.shape, lambda: (0, 0, 0)),
        out_shape=jax.ShapeDtypeStruct((N_HEADS_A,) + idx.shape, F32),
        name="bias_table",
    )(rel_bias, idx, add)


def _bucket(dist):
    max_exact = N_BUCKETS // 2
    d = jnp.maximum(dist, 1).astype(F32)
    large = max_exact + (jnp.log(d / max_exact) / math.log(BUCKET_MAX_DIST / max_exact)
                         * (N_BUCKETS - max_exact)).astype(jnp.int32)
    large = jnp.minimum(large, N_BUCKETS - 1)
    return jnp.where(dist < max_exact, dist, large)


def _prompt_bias_index(dilation, nk, first):
    i = jnp.arange(Q_BLOCK, dtype=jnp.int32)[:, None]
    j = jnp.arange(Q_BLOCK + nk, dtype=jnp.int32)[None, :]
    step = i - j + nk
    valid = (step >= 0) & (step <= nk)
    if first:
        valid = valid & (j >= nk)
    idx = _bucket(jnp.clip(step, 0, nk) * dilation)
    return idx, jnp.where(valid, 0.0, NEG_INF).astype(F32)


def _sample_bias_index(delta):
    count = jnp.zeros(delta.shape, jnp.int32)
    for window, dil in DILATED_BRANCHES:
        count += ((delta >= 0) & (delta % dil == 0) & (delta <= window)).astype(jnp.int32)
    add = jnp.where(count > 0, jnp.log(jnp.maximum(count, 1).astype(F32)), NEG_INF)
    return _bucket(jnp.maximum(delta, 0)), add.astype(F32)


def _attn_prompt_body(q_ref, k_ref, v_ref, bias_ref, o_ref, qs, ks, vs, lses, outs, pf, *, dilations, base):
    seq = q_ref.shape[1]
    nblk = seq // Q_BLOCK
    lane = lax.broadcasted_iota(jnp.int32, (Q_BLOCK, LANES), 1)
    lo = lane < HEAD_DIM
    lane_row = lax.broadcasted_iota(jnp.int32, (1, LANES), 1)
    head_sel = (jnp.where(lane_row < HEAD_DIM, 1.0, 0.0).astype(BF16),
                jnp.where(lane_row < HEAD_DIM, 0.0, 1.0).astype(BF16))
    zeros_blk = jnp.zeros((Q_BLOCK, LANES), BF16)
    vs[:, LANES:] = jnp.ones((vs.shape[0], LANES), BF16)

    lb = seq // base
    for xi, x_ref in enumerate((q_ref, k_ref, v_ref)):
        for r in range(base):
            pf[xi, pl.ds(r * lb, lb), :] = x_ref[0, pl.ds(r, lb, stride=base), :]

    def subsequence(xi, x_ref, dil, r):
        ln = seq // dil
        if dil == 1:
            return x_ref[0]
        if dil == base:
            return pf[xi, pl.ds(r * lb, ln), :]
        assert dil % base == 0
        return pf[xi, pl.ds((r % base) * lb + r // base, ln, stride=dil // base), :]

    for bi, dil in enumerate(dilations):
        ln = seq // dil
        nqb = ln // Q_BLOCK
        kstride = ln + Q_BLOCK
        for r in range(dil):
            qs[pl.ds(r * ln, ln), :] = subsequence(0, q_ref, dil, r).astype(BF16)
            ks[pl.ds(r * kstride, Q_BLOCK), :] = zeros_blk
            vs[pl.ds(r * kstride, Q_BLOCK), 0:LANES] = zeros_blk
            ks[pl.ds(r * kstride + Q_BLOCK, ln), :] = subsequence(1, k_ref, dil, r).astype(BF16)
            vs[pl.ds(r * kstride + Q_BLOCK, ln), 0:LANES] = subsequence(2, v_ref, dil, r).astype(BF16)
        last = bi == len(dilations) - 1

        def block(ib, carry, bi=bi, dil=dil, ln=ln, nqb=nqb, kstride=kstride, last=last):
            r = ib // nqb
            qb = ib % nqb
            qrow = pl.multiple_of(r * ln + qb * Q_BLOCK, Q_BLOCK)
            krow = pl.multiple_of(r * kstride + qb * Q_BLOCK, Q_BLOCK)
            qblk = qs[pl.ds(qrow, Q_BLOCK), :]
            kblk = ks[pl.ds(krow, 2 * Q_BLOCK), :]
            vblk = vs[pl.ds(krow, 2 * Q_BLOCK), :]
            first_blk = jnp.asarray(qb == 0, jnp.int32)
            q2 = jnp.concatenate([qblk * head_sel[0], qblk * head_sel[1]], axis=0)
            s2 = lax.dot_general(q2, kblk, NT_DIMS, preferred_element_type=F32)
            ms_h, ps_h = [], []
            for hh in range(2):
                s = s2[hh * Q_BLOCK:(hh + 1) * Q_BLOCK] + bias_ref[bi, first_blk, hh]
                m = jnp.max(s, axis=1, keepdims=True)
                ms_h.append(m)
                ps_h.append(jnp.exp(s - m).astype(BF16))
            pv = jnp.dot(jnp.concatenate(ps_h, axis=0), vblk, preferred_element_type=F32)
            m_p = jnp.where(lo, ms_h[0], ms_h[1])
            l_p = jnp.where(lo, pv[0:Q_BLOCK, LANES:], pv[Q_BLOCK:2 * Q_BLOCK, LANES:])
            acc = jnp.where(lo, pv[0:Q_BLOCK, 0:LANES], pv[Q_BLOCK:2 * Q_BLOCK, 0:LANES])
            if not last:
                dst = pl.ds(qb * (Q_BLOCK * dil) + r, Q_BLOCK, stride=dil)
                lses[bi, dst, :] = m_p + jnp.log(l_p)
                outs[bi, dst, :] = acc * (1.0 / l_p)
            else:
                rows = pl.ds(qrow, Q_BLOCK)
                m_all = m_p
                for bj in range(len(dilations) - 1):
                    m_all = jnp.maximum(m_all, lses[bj, rows, :])
                w = jnp.exp(m_p - m_all)
                den = w * l_p
                acc = w * acc
                for bj in range(len(dilations) - 1):
                    w = jnp.exp(lses[bj, rows, :] - m_all)
                    den = den + w
                    acc = acc + w * outs[bj, rows, :]
                o_ref[0, rows, :] = (acc / den).astype(BF16)
            return carry

        lax.fori_loop(0, nblk, block, 0, unroll=BLOCK_UNROLL)


def _attn_prompt(q, k, v, bias, *, batch, seq, dilations):
    assert dilations[-1] == 1
    dmax = max(dilations)
    base = min(d for d in dilations if d > 1)
    assert all(d % base == 0 for d in dilations if d > 1)
    v3 = lambda a: a.reshape(batch, seq, WIDTH_A)
    blk = pl.BlockSpec((1, seq, LANES), lambda b, hp: (b, 0, hp))
    nb = len(dilations)
    out = pl.pallas_call(
        functools.partial(_attn_prompt_body, dilations=dilations, base=base),
        grid=(batch, N_PAIRS),
        in_specs=[blk, blk, blk,
                  pl.BlockSpec((nb, 2, 2, Q_BLOCK, 2 * Q_BLOCK), lambda b, hp: (0, 0, hp, 0, 0))],
        out_specs=blk,
        out_shape=jax.ShapeDtypeStruct((batch, seq, WIDTH_A), BF16),
        scratch_shapes=[pltpu.VMEM((seq, LANES), BF16),
                        pltpu.VMEM((seq + dmax * Q_BLOCK, LANES), BF16),
                        pltpu.VMEM((seq + dmax * Q_BLOCK, 2 * LANES), BF16),
                        pltpu.VMEM((nb - 1, seq, LANES), F32),
                        pltpu.VMEM((nb - 1, seq, LANES), F32),
                        pltpu.VMEM((3, seq, LANES), F32)],
        compiler_params=_cparams(("arbitrary", "arbitrary")),
        name="attn_prompt",
    )(v3(q), v3(k), v3(v), bias)
    return out.reshape(batch * seq, WIDTH_A)


def _shift_window(c_ref, n_ref, w_ref, hp, t_new, seq_id):
    w_buf = c_ref.shape[3]
    is_new = lax.broadcasted_iota(jnp.int32, (LANES, LANES), 1) >= LANES - t_new
    new_shift = (LANES - t_new) - t_new * (seq_id % (LANES // t_new))
    x = c_ref[0, hp]
    rolled = pltpu.roll(x, w_buf - t_new, axis=1)
    new = pltpu.roll(n_ref[pl.ds(hp * LANES, LANES), :], new_shift, axis=1)
    w_ref[0, hp, :, 0:w_buf - LANES] = rolled[:, 0:w_buf - LANES]
    w_ref[0, hp, :, w_buf - LANES:w_buf] = jnp.where(is_new, new, rolled[:, w_buf - LANES:w_buf])
    return w_ref[0, hp].astype(BF16), x[:, 0:LANES].astype(BF16)


def _sample_keys(q_ref, kn_ref, ck_ref, bw_ref, bd_ref, wk_ref, pw_ref, pd_ref, *, t_new, seq_id):
    lo = lax.broadcasted_iota(jnp.int32, (t_new, LANES), 1) < HEAD_DIM
    for hp in range(ck_ref.shape[1]):
        q = q_ref[0, :, pl.ds(hp * LANES, LANES)]
        zero = jnp.zeros_like(q)
        q2 = jnp.concatenate([jnp.where(lo, q, zero), jnp.where(lo, zero, q)], axis=0).astype(BF16)
        kw, kd = _shift_window(ck_ref, kn_ref, wk_ref, hp, t_new, seq_id)
        yield
        s_w = jnp.dot(q2, kw, preferred_element_type=F32) + bw_ref[hp]
        s_d = jnp.dot(q2, kd, preferred_element_type=F32) + bd_ref[hp]
        m = jnp.maximum(jnp.max(s_w, axis=1, keepdims=True), jnp.max(s_d, axis=1, keepdims=True))
        p_w = jnp.exp(s_w - m)
        p_d = jnp.exp(s_d - m)
        inv = 1.0 / (jnp.sum(p_w, axis=1, keepdims=True) + jnp.sum(p_d, axis=1, keepdims=True))
        pw_ref[0, hp] = (p_w * inv).astype(BF16)
        pd_ref[0, hp] = (p_d * inv).astype(BF16)
        yield


def _sample_values(pw_ref, pd_ref, vn_ref, cv_ref, wv_ref, o_ref, *, t_new, seq_id):
    lo = lax.broadcasted_iota(jnp.int32, (t_new, LANES), 1) < HEAD_DIM
    for hp in range(cv_ref.shape[1]):
        vw, vd = _shift_window(cv_ref, vn_ref, wv_ref, hp, t_new, seq_id)
        yield
        o = (lax.dot_general(pw_ref[0, hp], vw, NT_DIMS, preferred_element_type=F32)
             + lax.dot_general(pd_ref[0, hp], vd, NT_DIMS, preferred_element_type=F32))
        o_ref[0, :, pl.ds(hp * LANES, LANES)] = jnp.where(lo, o[0:t_new], o[t_new:2 * t_new]).astype(BF16)
        yield


def _gla_rows(q, k, g, v, chunk, get_state, put_state, o_ref, rsel_ref, a_ref):
    r = q.shape[0]
    ng = r // chunk
    row = lax.broadcasted_iota(jnp.int32, (r, WIDTH_BK), 0)
    pos = row % chunk
    b = g
    sh = 1
    while sh < chunk:
        b = b + jnp.where(pos >= sh, pltpu.roll(b, sh, axis=0), 0.0)
        sh *= 2
    b3 = b.reshape(ng, chunk, WIDTH_BK)
    q3 = q.reshape(ng, chunk, WIDTH_BK)
    k3 = k.reshape(ng, chunk, WIDTH_BK)
    bl3 = jnp.broadcast_to(b3[:, chunk - 1:chunk, :], b3.shape)
    bl = bl3.reshape(r, WIDTH_BK)

    gi = lax.broadcasted_iota(jnp.int32, (WIDTH_BK, WIDTH_BV), 0) // DK_B
    gj = lax.broadcasted_iota(jnp.int32, (WIDTH_BK, WIDTH_BV), 1) // DV_B
    expand = jnp.where(gi == gj, 1.0, 0.0).astype(BF16)
    di = lax.broadcasted_iota(jnp.int32, (WIDTH_BV, WIDTH_BK), 0) // DV_B
    dj = lax.broadcasted_iota(jnp.int32, (WIDTH_BV, WIDTH_BK), 1) // DK_B
    diag = di == dj

    nsub = GLA_SUB // chunk
    s_idx = lax.broadcasted_iota(jnp.int32, (nsub, chunk, WIDTH_BK), 1)
    o_intra = []
    for sb in range(r // GLA_SUB):
        gs = slice(sb * nsub, (sb + 1) * nsub)
        for t in range(chunk):
            dec = jnp.exp(jnp.where(s_idx <= t, b3[gs, t:t + 1, :] - b3[gs], NEG_INF))
            a = dec * k3[gs] * q3[gs, t:t + 1, :]
            a_ref[sb, pl.ds(t * GLA_SUB, GLA_SUB), :] = a.reshape(GLA_SUB, WIDTH_BK).astype(BF16)
        w = jnp.dot(a_ref[sb], expand, preferred_element_type=F32)
        wv = w.reshape(chunk, GLA_SUB, WIDTH_BV) * v[sb * GLA_SUB:(sb + 1) * GLA_SUB][None]
        o_intra.append(jnp.dot(rsel_ref[...], wv.reshape(chunk * GLA_SUB, WIDTH_BV).astype(BF16),
                               preferred_element_type=F32))

    qd = (q * jnp.exp(b)).astype(BF16)
    kd = (k * jnp.exp(bl - b)).astype(BF16)
    gdec = jnp.exp(bl)
    vt = v.T.astype(BF16)
    grp = lax.broadcasted_iota(jnp.int32, (GLA_SUB, WIDTH_BK), 0) // chunk
    for j in range(ng):
        st = get_state(j)
        rows = slice(j * chunk, (j + 1) * chunk)
        sb, jl = divmod(j, nsub)
        sub = slice(sb * GLA_SUB, (sb + 1) * GLA_SUB)
        o_inter = lax.dot_general(qd[rows], st.astype(BF16), NT_DIMS, preferred_element_type=F32)
        o_ref[rows, :] = o_intra[sb][jl * chunk:(jl + 1) * chunk] + o_inter
        kj = jnp.where(grp == jl, kd[sub], jnp.zeros((GLA_SUB, WIDTH_BK), BF16))
        ds = jnp.dot(vt[:, sub], kj, preferred_element_type=F32)
        put_state(j, st * gdec[j * chunk:j * chunk + 1, :] + jnp.where(diag, ds, 0.0))


def _gla_rsel(rows, chunk):
    n = jnp.arange(rows)[:, None]
    c = jnp.arange(chunk * rows)[None, :]
    return ((c // rows == n % chunk) & ((c % rows) // chunk == n // chunk)).astype(BF16)


def _gla_prompt_body(q_ref, k_ref, g_ref, v_ref, rsel_ref, o_ref, s_ref, st_ref, a_ref, *, chunk):
    @pl.when(pl.program_id(1) == 0)
    def _():
        st_ref[...] = jnp.zeros(st_ref.shape, F32)

    def put(j, s):
        st_ref[...] = s

    _gla_rows(q_ref[0], k_ref[0], g_ref[0], v_ref[0], chunk, lambda j: st_ref[...], put, o_ref.at[0],
              rsel_ref, a_ref)
    s_ref[0] = st_ref[...]


def _gla_prompt(q, k, g, v, *, batch, seq, rows=512):
    chunk = math.gcd(seq, GLA_CHUNK)
    rows = min(rows, seq)
    v3 = lambda a: a.reshape(batch, seq, a.shape[-1])
    blk = lambda w: pl.BlockSpec((1, rows, w), lambda b, i: (b, i, 0))
    o, st = pl.pallas_call(
        functools.partial(_gla_prompt_body, chunk=chunk),
        grid=(batch, seq // rows),
        in_specs=[blk(WIDTH_BK), blk(WIDTH_BK), blk(WIDTH_BK), blk(WIDTH_BV),
                  _const_spec((GLA_SUB, chunk * GLA_SUB))],
        out_specs=[blk(WIDTH_BV), pl.BlockSpec((1, WIDTH_BV, WIDTH_BK), lambda b, i: (b, 0, 0))],
        out_shape=[jax.ShapeDtypeStruct((batch, seq, WIDTH_BV), F32),
                   jax.ShapeDtypeStruct((batch, WIDTH_BV, WIDTH_BK), F32)],
        scratch_shapes=[pltpu.VMEM((WIDTH_BV, WIDTH_BK), F32), pltpu.VMEM((rows // GLA_SUB, chunk * GLA_SUB, WIDTH_BK), BF16)],
        compiler_params=_cparams(("arbitrary", "arbitrary")),
        name="gla_prompt",
    )(v3(q), v3(k), v3(g), v3(v), _gla_rsel(GLA_SUB, chunk))
    return o.reshape(batch * seq, WIDTH_BV), st


def _gla_sample_body(q_ref, k_ref, g_ref, v_ref, s0_ref, rsel_ref, o_ref, s1_ref, a_ref, *, chunk):
    def put(j, s):
        s1_ref[j] = s

    _gla_rows(q_ref[...], k_ref[...], g_ref[...], v_ref[...], chunk, lambda j: s0_ref[j], put, o_ref,
              rsel_ref, a_ref)


def _gla_sample(q, k, g, v, st0, *, t_new, rows=128):
    t = q.shape[0]
    rows = min(rows, t)
    nb = rows // t_new
    blk = lambda w: pl.BlockSpec((rows, w), lambda i: (i, 0))
    sblk = pl.BlockSpec((nb, WIDTH_BV, WIDTH_BK), lambda i: (i, 0, 0))
    return pl.pallas_call(
        functools.partial(_gla_sample_body, chunk=t_new),
        grid=(t // rows,),
        in_specs=[blk(WIDTH_BK), blk(WIDTH_BK), blk(WIDTH_BK), blk(WIDTH_BV), sblk,
                  _const_spec((GLA_SUB, t_new * GLA_SUB))],
        out_specs=[blk(WIDTH_BV), sblk],
        out_shape=[jax.ShapeDtypeStruct((t, WIDTH_BV), F32), jax.ShapeDtypeStruct(st0.shape, F32)],
        scratch_shapes=[pltpu.VMEM((rows // GLA_SUB, t_new * GLA_SUB, WIDTH_BK), BF16)],
        compiler_params=_cparams(("arbitrary",)),
        name="gla_sample",
    )(q, k, g, v, st0, _gla_rsel(GLA_SUB, t_new))


def _state_to_blockdiag(s):
    b = s.shape[0]
    eye = jnp.eye(N_HEADS_B, dtype=s.dtype)
    return jnp.einsum('bhkv,hg->bhvgk', s, eye).reshape(b, WIDTH_BV, WIDTH_BK)


def _state_from_blockdiag(st):
    b = st.shape[0]
    s5 = st.reshape(b, N_HEADS_B, DV_B, N_HEADS_B, DK_B)
    idx = jnp.arange(N_HEADS_B)
    return s5[:, idx, :, idx, :].transpose(1, 0, 3, 2)


def _outproj_body(y_ref, oa_ref, ob_ref, rb_ref, gg_ref, m64_ref, wa_ref, wb_ref, o_ref):
    ob = ob_ref[...]
    ms = jnp.dot((ob * ob).astype(BF16), m64_ref[...], preferred_element_type=F32)
    rb = rb_ref[...]
    gated = ob * lax.rsqrt(ms + EPS) * gg_ref[...] * (rb * jax.nn.sigmoid(rb))
    o_ref[...] = (y_ref[...]
                  + jnp.dot(oa_ref[...], wa_ref[...], preferred_element_type=F32)
                  + jnp.dot(gated.astype(BF16), wb_ref[...], preferred_element_type=F32))


def _outproj(y, oa, ob, rb, gg, m64, wa, wb, *, tm=512):
    t, d = y.shape
    tm = min(tm, t)
    row = lambda w: pl.BlockSpec((tm, w), lambda i: (i, 0))
    return pl.pallas_call(
        _outproj_body,
        grid=(t // tm,),
        in_specs=[row(d), row(WIDTH_A), row(WIDTH_BV), row(WIDTH_BV), _const_spec((1, WIDTH_BV)),
                  _const_spec((WIDTH_BV, WIDTH_BV)), _const_spec((WIDTH_A, d)), _const_spec((WIDTH_BV, d))],
        out_specs=row(d),
        out_shape=jax.ShapeDtypeStruct((t, d), F32),
        compiler_params=_cparams(("arbitrary",)),
        name="outproj",
    )(y, oa, ob, rb, gg, m64, wa, wb)


def _head_mean_matrix(width):
    i = jnp.arange(width) // HEAD_DIM
    return jnp.where(i[:, None] == i[None, :], 1.0 / HEAD_DIM, 0.0).astype(BF16)


def kernel(x_prompt, x_sample, cache_win_k, cache_win_v, state_gla, ffn1_norm, ffn1_w1, ffn1_w3, ffn1_w2,
           mix_norm, w_in, q_norm, k_norm, rel_bias, w_gk2, b_gk, gla_norm, w_out, ffn2_norm, ffn2_w1,
           ffn2_w3, ffn2_w2):
    batch, seq, d_model = x_prompt.shape
    dec_batch, dec_seq, _ = x_sample.shape
    depth = ffn1_w1.shape[0]
    w_buf = cache_win_k.shape[2]
    dilations = tuple(sorted((d for _, d in DILATED_BRANCHES), reverse=True))
    assert seq % (Q_BLOCK * dilations[0]) == 0
    assert all(w // d == Q_BLOCK for w, d in DILATED_BRANCHES)
    assert GLA_CHUNK % dec_seq == 0 and LANES % dec_seq == 0

    assert WIDTH_BV == MXU_N and 2 * WIDTH_BK == MXU_N and WIDTH_A % MXU_N == 0
    m64_a = _head_mean_matrix(MXU_N)
    m64_b = _head_mean_matrix(WIDTH_BV)
    tile2 = lambda g: jnp.tile(g, MXU_N // HEAD_DIM)[None, :]

    stage_bias = []
    for dil in dilations:
        i0, a0 = _prompt_bias_index(dil, Q_BLOCK, first=False)
        i1, a1 = _prompt_bias_index(dil, Q_BLOCK, first=True)
        tbl = _bias_table(rel_bias, jnp.concatenate([i0, i1], 0), jnp.concatenate([a0, a1], 0))
        stage_bias.append(tbl.reshape(N_HEADS_A, 2, Q_BLOCK, 2 * Q_BLOCK).transpose(1, 0, 2, 3))
    stage_bias = jnp.stack(stage_bias)
    qi = jnp.arange(dec_seq, dtype=jnp.int32)[:, None]
    sw_idx, sw_add = _sample_bias_index(w_buf - dec_seq + qi - jnp.arange(w_buf, dtype=jnp.int32)[None, :])
    sd_idx, sd_add = _sample_bias_index(w_buf + qi - jnp.arange(LANES, dtype=jnp.int32)[None, :])
    sd_add = jnp.where(jnp.arange(LANES)[None, :] < dec_seq, sd_add, NEG_INF)
    pair_rows = lambda t: t.reshape(N_PAIRS, 2 * dec_seq, t.shape[-1])
    bias_w = pair_rows(_bias_table(rel_bias, sw_idx, sw_add))
    bias_d = pair_rows(_bias_table(rel_bias, sd_idx, sd_add))

    yp = x_prompt.reshape(batch * seq, d_model)
    ys = x_sample.reshape(dec_batch * dec_seq, d_model)
    outs = [[] for _ in range(6)]
    for l in range(depth):
        bf = lambda w: w.astype(BF16)
        f1 = (ffn1_norm[l][None, :], bf(ffn1_w1[l]), bf(ffn1_w3[l]), bf(ffn1_w2[l]))
        f2 = (ffn2_norm[l][None, :], bf(ffn2_w1[l]), bf(ffn2_w3[l]), bf(ffn2_w2[l]))
        w_pad = jnp.pad(bf(w_in[l]), ((0, 0), (0, PROJ_PAD - w_in.shape[2])))
        wgk = jnp.pad(bf(w_gk2[l]), ((0, MXU_N - GATE_RANK), (0, 0)))
        pj = (mix_norm[l][None, :], w_pad, tile2(q_norm[l]), tile2(k_norm[l]), m64_a, wgk, b_gk[l][None, :])
        op = (jnp.tile(gla_norm[l], N_HEADS_B)[None, :], m64_b, bf(w_out[l][:WIDTH_A]), bf(w_out[l][WIDTH_A:]))

        ys1 = _ffn(ys, *f1)
        n_new = dec_batch * dec_seq
        qa_s, _, _, qb_s, kb_s, vb_s, rb_s, gk_s, kt_s, vt_s = _proj(ys1, *pj, seq=n_new, n_keep=n_new)
        lane_major = lambda c: c.transpose(0, 2, 3, 1).reshape(dec_batch, N_PAIRS, LANES, w_buf)
        ck, cv = lane_major(cache_win_k[l]), lane_major(cache_win_v[l])
        sp = _sample_specs(dec_seq, w_buf)
        win_shape = jax.ShapeDtypeStruct(ck.shape, F32)

        y1, wk, p_w, p_d = _ffn_side(
            yp, *f1, functools.partial(_sample_keys, t_new=dec_seq),
            (qa_s.reshape(dec_batch, dec_seq, WIDTH_A), kt_s[0], ck, bias_w, bias_d),
            (sp['q'], sp['new'], sp['win'], _const_spec(bias_w.shape), _const_spec(bias_d.shape)),
            (sp['win'], sp['pw'], sp['pd']),
            (win_shape, jax.ShapeDtypeStruct((dec_batch,) + bias_w.shape, BF16),
             jax.ShapeDtypeStruct((dec_batch,) + bias_d.shape, BF16)),
            steps=dec_batch)
        n_keep = min(WIN_MAX, seq)
        qa, ka, va, qb, kb, vb, rb, gk, kt, vt = _proj(y1, *pj, seq=seq, n_keep=n_keep)
        oa = _attn_prompt(qa, ka, va, stage_bias, batch=batch, seq=seq, dilations=dilations)
        ob, st = _gla_prompt(qb, kb, gk, vb, batch=batch, seq=seq)
        yp, wv, oa_s = _ffn_side(
            _outproj(y1, oa, ob, rb, *op), *f2, functools.partial(_sample_values, t_new=dec_seq),
            (p_w, p_d, vt_s[0], cv), (sp['pw'], sp['pd'], sp['new'], sp['win']), (sp['win'], sp['q']),
            (win_shape, jax.ShapeDtypeStruct((dec_batch, dec_seq, WIDTH_A), BF16)),
            steps=dec_batch)
        keep_major = lambda a: a.reshape(batch, N_HEADS_A, HEAD_DIM, n_keep).transpose(0, 3, 1, 2)
        outs[0].append(keep_major(kt))
        outs[1].append(keep_major(vt))
        outs[2].append(_state_from_blockdiag(st))

        ob, st = _gla_sample(qb_s, kb_s, gk_s, vb_s, _state_to_blockdiag(state_gla[l]), t_new=dec_seq)
        ys = _ffn(_outproj(ys1, oa_s.reshape(dec_batch * dec_seq, WIDTH_A), ob, rb_s, *op), *f2)
        row_major = lambda w: w.reshape(dec_batch, N_HEADS_A, HEAD_DIM, w_buf).transpose(0, 3, 1, 2)
        outs[3].append(row_major(wk))
        outs[4].append(row_major(wv))
        outs[5].append(_state_from_blockdiag(st))

    return (yp.reshape(batch, seq, d_model), ys.reshape(dec_batch, dec_seq, d_model),
            *(jnp.stack(o) for o in outs))
```

```python
import functools
import math

import jax
import jax.numpy as jnp
from jax import lax
from jax.experimental import pallas as pl
from jax.experimental.pallas import tpu as pltpu

F32 = jnp.float32
BF16 = jnp.bfloat16

HEAD_DIM = 64
N_HEADS_A = 12
N_HEADS_B = 4
DK_B = 32
DV_B = 64
GATE_RANK = 16
GATE_NORM = 16.0
GLA_CHUNK = 16
GLA_SUB = 128
DILATED_BRANCHES = ((128, 1), (512, 4), (2048, 16))
WIN_MAX = 2048
Q_BLOCK = 128
N_BUCKETS = 32
BUCKET_MAX_DIST = 2048
EPS = 1e-6
WIDTH_A = N_HEADS_A * HEAD_DIM
WIDTH_BK = N_HEADS_B * DK_B
WIDTH_BV = N_HEADS_B * DV_B

LANES = 128
MXU_N = 256
N_PAIRS = WIDTH_A // LANES
BLOCK_UNROLL = 16
VMEM_LIMIT = 56 * 1024 * 1024
NEG_INF = float("-inf")
NT_DIMS = (((1,), (1,)), ((), ()))


def _cparams(sem):
    return pltpu.CompilerParams(dimension_semantics=sem, vmem_limit_bytes=VMEM_LIMIT)


def _const_spec(shape):
    nd = len(shape)
    return pl.BlockSpec(shape, lambda *_: (0,) * nd, pipeline_mode=pl.Buffered(1))


def _rms_rows(x, gain):
    return x * lax.rsqrt(jnp.mean(x * x, axis=-1, keepdims=True) + EPS) * gain


def _ffn_tile(x_ref, g_ref, w1_ref, w3_ref, w2_ref, o_ref, act_ref, fc):
    x = x_ref[...]
    h = _rms_rows(x, g_ref[...]).astype(BF16)
    for c in range(act_ref.shape[1] // fc):
        sl = pl.ds(c * fc, fc)
        a = jnp.dot(h, w1_ref[:, sl], preferred_element_type=F32)
        b = jnp.dot(h, w3_ref[:, sl], preferred_element_type=F32)
        act_ref[:, sl] = (a * jax.nn.sigmoid(a) * b).astype(BF16)
        yield
    act = act_ref[...]
    for c in range(o_ref.shape[1] // fc):
        sl = pl.ds(c * fc, fc)
        o_ref[:, sl] = x_ref[:, sl] + 0.5 * jnp.dot(act, w2_ref[:, sl], preferred_element_type=F32)
        yield


def _ffn_body(x_ref, g_ref, w1_ref, w3_ref, w2_ref, o_ref, act_ref, *, fc):
    for _ in _ffn_tile(x_ref, g_ref, w1_ref, w3_ref, w2_ref, o_ref, act_ref, fc):
        pass


def _ffn(x, gain, w1, w3, w2, *, tm=512, fc=256):
    t, d = x.shape
    f = w1.shape[1]
    tm = min(tm, t)
    return pl.pallas_call(
        functools.partial(_ffn_body, fc=fc),
        grid=(t // tm,),
        in_specs=[pl.BlockSpec((tm, d), lambda i: (i, 0)),
                  _const_spec((1, d)), _const_spec((d, f)), _const_spec((d, f)), _const_spec((f, d))],
        out_specs=pl.BlockSpec((tm, d), lambda i: (i, 0)),
        out_shape=jax.ShapeDtypeStruct((t, d), F32),
        scratch_shapes=[pltpu.VMEM((tm, f), BF16)],
        compiler_params=_cparams(("arbitrary",)),
        name="ffn",
    )(x, gain, w1, w3, w2)


def _ffn_side_body(*refs, fc, side, side_pieces, n_side_in, n_side_out):
    ffn_in, refs = refs[:5], refs[5:]
    side_in, refs = refs[:n_side_in], refs[n_side_in:]
    o_ref, side_out, act_ref = refs[0], refs[1:1 + n_side_out], refs[1 + n_side_out]
    side_gen = side(*side_in, *side_out, seq_id=pl.program_id(0))
    n_main = (act_ref.shape[1] + o_ref.shape[1]) // fc
    done = 0
    for i, _ in enumerate(_ffn_tile(*ffn_in, o_ref, act_ref, fc)):
        while done < ((i + 1) * side_pieces) // n_main:
            next(side_gen)
            done += 1
    assert done == side_pieces and next(side_gen, "exhausted") == "exhausted"


def _ffn_side(x, gain, w1, w3, w2, side, side_ins, side_in_specs, side_out_specs, side_out_shape,
              *, steps, side_pieces, fc=256):
    t, d = x.shape
    f = w1.shape[1]
    assert t % steps == 0
    tm = t // steps
    assert tm % 8 == 0
    row = pl.BlockSpec((tm, d), lambda i: (i, 0))
    return pl.pallas_call(
        functools.partial(_ffn_side_body, fc=fc, side=side, side_pieces=side_pieces,
                          n_side_in=len(side_ins), n_side_out=len(side_out_specs)),
        grid=(steps,),
        in_specs=[row, _const_spec((1, d)), _const_spec((d, f)), _const_spec((d, f)), _const_spec((f, d)),
                  *side_in_specs],
        out_specs=[row, *side_out_specs],
        out_shape=[jax.ShapeDtypeStruct((t, d), F32), *side_out_shape],
        scratch_shapes=[pltpu.VMEM((tm, f), BF16)],
        compiler_params=_cparams(("arbitrary",)),
        name="ffn_side",
    )(x, gain, w1, w3, w2, *side_ins)


def _sample_specs(t_new, w_buf):
    per_tile = LANES // t_new
    return dict(
        q=pl.BlockSpec((1, t_new, WIDTH_A), lambda i: (i, 0, 0)),
        new=pl.BlockSpec((WIDTH_A, LANES), lambda i: (0, i // per_tile)),
        win=pl.BlockSpec((1, N_PAIRS, LANES, w_buf), lambda i: (i, 0, 0, 0)),
        pw=pl.BlockSpec((1, N_PAIRS, 2 * t_new, w_buf), lambda i: (i, 0, 0, 0)),
        pd=pl.BlockSpec((1, N_PAIRS, 2 * t_new, LANES), lambda i: (i, 0, 0, 0)),
    )


_OFF_Q, _OFF_K, _OFF_V = 0, WIDTH_A, 2 * WIDTH_A
_OFF_QB = 3 * WIDTH_A
_OFF_KB = _OFF_QB + WIDTH_BK
_OFF_VB = _OFF_KB + WIDTH_BK
_OFF_RB = _OFF_VB + WIDTH_BV
_OFF_GL = _OFF_RB + WIDTH_BV
PROJ_PAD = _OFF_GL + MXU_N


def _proj_body(y_ref, g_ref, w_ref, qg_ref, kg_ref, m64_ref, wgk_ref, bgk_ref,
               qa_ref, ka_ref, va_ref, qb_ref, kb_ref, vb_ref, rb_ref, gk_ref, *kv_t_refs, keep):
    h = _rms_rows(y_ref[...], g_ref[...]).astype(BF16)
    if keep is not None:
        kt_ref, vt_ref = kv_t_refs
        in_keep = pl.program_id(0) % keep[0] >= keep[1]

    def cols(off):
        return jnp.dot(h, w_ref[:, pl.ds(off, MXU_N)], preferred_element_type=F32)

    groups = [pl.ds(gi * MXU_N, MXU_N) for gi in range(WIDTH_A // MXU_N)]
    for gi, sl in enumerate(groups):
        qa_ref[:, sl] = cols(_OFF_Q + gi * MXU_N)
        ka_ref[:, sl] = cols(_OFF_K + gi * MXU_N)
        va_ref[:, sl] = cols(_OFF_V + gi * MXU_N)
    qkb = cols(_OFF_QB)
    qb_ref[...] = qkb[:, 0:WIDTH_BK] * (DK_B ** -0.5)
    kb_ref[...] = qkb[:, WIDTH_BK:2 * WIDTH_BK]
    vb_ref[...] = cols(_OFF_VB)
    rb_ref[...] = cols(_OFF_RB)
    glr = cols(_OFF_GL).astype(BF16)

    m64 = m64_ref[...]
    for sl in groups:
        q = qa_ref[:, sl]
        ms = jnp.dot((q * q).astype(BF16), m64, preferred_element_type=F32)
        qa_ref[:, sl] = q * lax.rsqrt(ms + EPS) * qg_ref[...] * (HEAD_DIM ** -0.5)
        k = ka_ref[:, sl]
        ms = jnp.dot((k * k).astype(BF16), m64, preferred_element_type=F32)
        kn = k * lax.rsqrt(ms + EPS) * kg_ref[...]
        ka_ref[:, sl] = kn
        if keep is not None:
            @pl.when(in_keep)
            def _(kn=kn, sl=sl):
                kt_ref[0, sl, :] = kn.T
                vt_ref[0, sl, :] = va_ref[:, sl].T
    xg = jnp.dot(glr, wgk_ref[...], preferred_element_type=F32) + bgk_ref[...]
    gk_ref[...] = (jnp.minimum(xg, 0.0) - jnp.log(1.0 + jnp.exp(-jnp.abs(xg)))) * (1.0 / GATE_NORM)


def _proj(y, gain, w_pad, qg, kg, m64, wgk, bgk, *, tm=512, seq=None, n_keep=None):
    t, d = y.shape
    tm = min(tm, t)
    row = lambda w: pl.BlockSpec((tm, w), lambda i: (i, 0))
    widths = (WIDTH_A, WIDTH_A, WIDTH_A, WIDTH_BK, WIDTH_BK, WIDTH_BV, WIDTH_BV, WIDTH_BK)
    out_specs = [row(w) for w in widths]
    out_shape = [jax.ShapeDtypeStruct((t, w), F32) for w in widths]
    keep = None
    if n_keep is not None:
        assert seq % tm == 0 and n_keep % tm == 0
        tps = seq // tm
        keep = (tps, (seq - n_keep) // tm)
        tail = pl.BlockSpec((1, WIDTH_A, tm), lambda i: (i // tps, 0, jnp.maximum(i % tps - keep[1], 0)))
        out_specs += [tail, tail]
        out_shape += [jax.ShapeDtypeStruct((t // seq, WIDTH_A, n_keep), F32)] * 2
    return pl.pallas_call(
        functools.partial(_proj_body, keep=keep),
        grid=(t // tm,),
        in_specs=[row(d), _const_spec((1, d)), _const_spec(w_pad.shape),
                  _const_spec((1, MXU_N)), _const_spec((1, MXU_N)), _const_spec((MXU_N, MXU_N)),
                  _const_spec((MXU_N, WIDTH_BK)), _const_spec((1, WIDTH_BK))],
        out_specs=out_specs,
        out_shape=out_shape,
        compiler_params=_cparams(("arbitrary",)),
        name="proj",
    )(y, gain, w_pad, qg, kg, m64, wgk, bgk)


def _bias_body(rb_ref, idx_ref, add_ref, o_ref):
    idx = idx_ref[...]
    add = add_ref[...]
    for h in range(N_HEADS_A):
        acc = jnp.zeros(idx.shape, F32)
        for b in range(N_BUCKETS):
            acc = jnp.where(idx == b, rb_ref[b, h], acc)
        o_ref[h] = acc + add


def _bias_table(rel_bias, idx, add):
    return pl.pallas_call(
        _bias_body,
        in_specs=[pl.BlockSpec(memory_space=pltpu.SMEM),
                  pl.BlockSpec(idx.shape, lambda: (0, 0)), pl.BlockSpec(idx.shape, lambda: (0, 0))],
        out_specs=pl.BlockSpec((N_HEADS_A,) + idx.shape, lambda: (0, 0, 0)),
        out_shape=jax.ShapeDtypeStruct((N_HEADS_A,) + idx.shape, F32),
        name="bias_table",
    )(rel_bias, idx, add)


def _bucket(dist):
    max_exact = N_BUCKETS // 2
    d = jnp.maximum(dist, 1).astype(F32)
    large = max_exact + (jnp.log(d / max_exact) / math.log(BUCKET_MAX_DIST / max_exact)
                         * (N_BUCKETS - max_exact)).astype(jnp.int32)
    large = jnp.minimum(large, N_BUCKETS - 1)
    return jnp.where(dist < max_exact, dist, large)


def _prompt_bias_index(dilation, nk, first):
    i = jnp.arange(Q_BLOCK, dtype=jnp.int32)[:, None]
    j = jnp.arange(Q_BLOCK + nk, dtype=jnp.int32)[None, :]
    step = i - j + nk
    valid = (step >= 0) & (step <= nk)
    if first:
        valid = valid & (j >= nk)
    idx = _bucket(jnp.clip(step, 0, nk) * dilation)
    return idx, jnp.where(valid, 0.0, NEG_INF).astype(F32)


def _sample_bias_index(delta):
    count = jnp.zeros(delta.shape, jnp.int32)
    for window, dil in DILATED_BRANCHES:
        count += ((delta >= 0) & (delta % dil == 0) & (delta <= window)).astype(jnp.int32)
    add = jnp.where(count > 0, jnp.log(jnp.maximum(count, 1).astype(F32)), NEG_INF)
    return _bucket(jnp.maximum(delta, 0)), add.astype(F32)


def _attn_prompt_body(q_ref, k_ref, v_ref, bias_ref, o_ref, qs, ks, vs, lses, outs, pf, *, dilations, base):
    seq = q_ref.shape[1]
    nblk = seq // Q_BLOCK
    lane = lax.broadcasted_iota(jnp.int32, (Q_BLOCK, LANES), 1)
    lo = lane < HEAD_DIM
    lane_row = lax.broadcasted_iota(jnp.int32, (1, LANES), 1)
    head_sel = (jnp.where(lane_row < HEAD_DIM, 1.0, 0.0).astype(BF16),
                jnp.where(lane_row < HEAD_DIM, 0.0, 1.0).astype(BF16))
    zeros_blk = jnp.zeros((Q_BLOCK, LANES), BF16)
    vs[:, LANES:] = jnp.ones((vs.shape[0], LANES), BF16)

    lb = seq // base
    for xi, x_ref in enumerate((q_ref, k_ref, v_ref)):
        for r in range(base):
            pf[xi, pl.ds(r * lb, lb), :] = x_ref[0, pl.ds(r, lb, stride=base), :]

    def subsequence(xi, x_ref, dil, r):
        ln = seq // dil
        if dil == 1:
            return x_ref[0]
        if dil == base:
            return pf[xi, pl.ds(r * lb, ln), :]
        assert dil % base == 0
        return pf[xi, pl.ds((r % base) * lb + r // base, ln, stride=dil // base), :]

    for bi, dil in enumerate(dilations):
        ln = seq // dil
        nqb = ln // Q_BLOCK
        kstride = ln + Q_BLOCK
        for r in range(dil):
            qs[pl.ds(r * ln, ln), :] = subsequence(0, q_ref, dil, r).astype(BF16)
            ks[pl.ds(r * kstride, Q_BLOCK), :] = zeros_blk
            vs[pl.ds(r * kstride, Q_BLOCK), 0:LANES] = zeros_blk
            ks[pl.ds(r * kstride + Q_BLOCK, ln), :] = subsequence(1, k_ref, dil, r).astype(BF16)
            vs[pl.ds(r * kstride + Q_BLOCK, ln), 0:LANES] = subsequence(2, v_ref, dil, r).astype(BF16)
        last = bi == len(dilations) - 1

        def block(ib, bi=bi, dil=dil, ln=ln, nqb=nqb, kstride=kstride, last=last):
            r = ib // nqb
            qb = ib % nqb
            qrow = pl.multiple_of(r * ln + qb * Q_BLOCK, Q_BLOCK)
            krow = pl.multiple_of(r * kstride + qb * Q_BLOCK, Q_BLOCK)
            qblk = qs[pl.ds(qrow, Q_BLOCK), :]
            kblk = ks[pl.ds(krow, 2 * Q_BLOCK), :]
            vblk = vs[pl.ds(krow, 2 * Q_BLOCK), :]
            first_blk = jnp.asarray(qb == 0, jnp.int32)
            q2 = jnp.concatenate([qblk * head_sel[0], qblk * head_sel[1]], axis=0)
            s2 = lax.dot_general(q2, kblk, NT_DIMS, preferred_element_type=F32)
            yield
            ms_h, ps_h = [], []
            for hh in range(2):
                s = s2[hh * Q_BLOCK:(hh + 1) * Q_BLOCK] + bias_ref[bi, first_blk, hh]
                m = jnp.max(s, axis=1, keepdims=True)
                ms_h.append(m)
                ps_h.append(jnp.exp(s - m).astype(BF16))
            pv = jnp.dot(jnp.concatenate(ps_h, axis=0), vblk, preferred_element_type=F32)
            m_p = jnp.where(lo, ms_h[0], ms_h[1])
            l_p = jnp.where(lo, pv[0:Q_BLOCK, LANES:], pv[Q_BLOCK:2 * Q_BLOCK, LANES:])
            acc = jnp.where(lo, pv[0:Q_BLOCK, 0:LANES], pv[Q_BLOCK:2 * Q_BLOCK, 0:LANES])
            if not last:
                dst = pl.ds(qb * (Q_BLOCK * dil) + r, Q_BLOCK, stride=dil)
                lses[bi, dst, :] = m_p + jnp.log(l_p)
                outs[bi, dst, :] = acc * (1.0 / l_p)
            else:
                rows = pl.ds(qrow, Q_BLOCK)
                m_all = m_p
                for bj in range(len(dilations) - 1):
                    m_all = jnp.maximum(m_all, lses[bj, rows, :])
                w = jnp.exp(m_p - m_all)
                den = w * l_p
                acc = w * acc
                for bj in range(len(dilations) - 1):
                    w = jnp.exp(lses[bj, rows, :] - m_all)
                    den = den + w
                    acc = acc + w * outs[bj, rows, :]
                o_ref[0, rows, :] = (acc / den).astype(BF16)

        def group(ig, carry, block=block):
            gens = [block(ig * BLOCK_UNROLL + u) for u in range(BLOCK_UNROLL)]
            next(gens[0])
            for u in range(BLOCK_UNROLL):
                if u + 1 < BLOCK_UNROLL:
                    next(gens[u + 1])
                assert next(gens[u], "exhausted") == "exhausted"
            return carry

        assert nblk % BLOCK_UNROLL == 0
        lax.fori_loop(0, nblk // BLOCK_UNROLL, group, 0)


def _attn_prompt(q, k, v, bias, *, batch, seq, dilations):
    assert dilations[-1] == 1
    dmax = max(dilations)
    base = min(d for d in dilations if d > 1)
    assert all(d % base == 0 for d in dilations if d > 1)
    v3 = lambda a: a.reshape(batch, seq, WIDTH_A)
    blk = pl.BlockSpec((1, seq, LANES), lambda b, hp: (b, 0, hp))
    nb = len(dilations)
    out = pl.pallas_call(
        functools.partial(_attn_prompt_body, dilations=dilations, base=base),
        grid=(batch, N_PAIRS),
        in_specs=[blk, blk, blk,
                  pl.BlockSpec((nb, 2, 2, Q_BLOCK, 2 * Q_BLOCK), lambda b, hp: (0, 0, hp, 0, 0))],
        out_specs=blk,
        out_shape=jax.ShapeDtypeStruct((batch, seq, WIDTH_A), BF16),
        scratch_shapes=[pltpu.VMEM((seq, LANES), BF16),
                        pltpu.VMEM((seq + dmax * Q_BLOCK, LANES), BF16),
                        pltpu.VMEM((seq + dmax * Q_BLOCK, 2 * LANES), BF16),
                        pltpu.VMEM((nb - 1, seq, LANES), F32),
                        pltpu.VMEM((nb - 1, seq, LANES), F32),
                        pltpu.VMEM((3, seq, LANES), F32)],
        compiler_params=_cparams(("arbitrary", "arbitrary")),
        name="attn_prompt",
    )(v3(q), v3(k), v3(v), bias)
    return out.reshape(batch * seq, WIDTH_A)


SHIFT_PIECES = 2


def _shift_window(c_ref, n_ref, w_ref, hp, t_new, seq_id, result):
    w_buf = c_ref.shape[3]
    is_new = lax.broadcasted_iota(jnp.int32, (LANES, LANES), 1) >= LANES - t_new
    new_shift = (LANES - t_new) - t_new * (seq_id % (LANES // t_new))
    x = c_ref[0, hp]
    rolled = pltpu.roll(x, w_buf - t_new, axis=1)
    new = pltpu.roll(n_ref[pl.ds(hp * LANES, LANES), :], new_shift, axis=1)
    half = w_buf // 2
    w_ref[0, hp, :, 0:half] = rolled[:, 0:half]
    yield
    w_ref[0, hp, :, half:w_buf - LANES] = rolled[:, half:w_buf - LANES]
    w_ref[0, hp, :, w_buf - LANES:w_buf] = jnp.where(is_new, new, rolled[:, w_buf - LANES:w_buf])
    result.extend((w_ref[0, hp].astype(BF16), x[:, 0:LANES].astype(BF16)))


def _sample_keys(q_ref, kn_ref, ck_ref, bw_ref, bd_ref, wk_ref, pw_ref, pd_ref, *, t_new, seq_id):
    lo = lax.broadcasted_iota(jnp.int32, (t_new, LANES), 1) < HEAD_DIM
    for hp in range(ck_ref.shape[1]):
        q = q_ref[0, :, pl.ds(hp * LANES, LANES)]
        zero = jnp.zeros_like(q)
        q2 = jnp.concatenate([jnp.where(lo, q, zero), jnp.where(lo, zero, q)], axis=0).astype(BF16)
        shifted = []
        yield from _shift_window(ck_ref, kn_ref, wk_ref, hp, t_new, seq_id, shifted)
        kw, kd = shifted
        yield
        s_w = jnp.dot(q2, kw, preferred_element_type=F32) + bw_ref[hp]
        s_d = jnp.dot(q2, kd, preferred_element_type=F32) + bd_ref[hp]
        m = jnp.maximum(jnp.max(s_w, axis=1, keepdims=True), jnp.max(s_d, axis=1, keepdims=True))
        p_w = jnp.exp(s_w - m)
        p_d = jnp.exp(s_d - m)
        inv = 1.0 / (jnp.sum(p_w, axis=1, keepdims=True) + jnp.sum(p_d, axis=1, keepdims=True))
        pw_ref[0, hp] = (p_w * inv).astype(BF16)
        pd_ref[0, hp] = (p_d * inv).astype(BF16)
        yield


def _sample_values(pw_ref, pd_ref, vn_ref, cv_ref, wv_ref, o_ref, *, t_new, seq_id):
    lo = lax.broadcasted_iota(jnp.int32, (t_new, LANES), 1) < HEAD_DIM
    for hp in range(cv_ref.shape[1]):
        shifted = []
        yield from _shift_window(cv_ref, vn_ref, wv_ref, hp, t_new, seq_id, shifted)
        vw, vd = shifted
        yield
        o = (lax.dot_general(pw_ref[0, hp], vw, NT_DIMS, preferred_element_type=F32)
             + lax.dot_general(pd_ref[0, hp], vd, NT_DIMS, preferred_element_type=F32))
        o_ref[0, :, pl.ds(hp * LANES, LANES)] = jnp.where(lo, o[0:t_new], o[t_new:2 * t_new]).astype(BF16)
        yield


def _gla_scratch(rows, chunk):
    ng = rows // chunk
    return [pltpu.VMEM((rows // GLA_SUB, chunk * GLA_SUB, WIDTH_BK), BF16),
            pltpu.VMEM((ng, WIDTH_BV, WIDTH_BK), F32), pltpu.VMEM((ng, WIDTH_BV, WIDTH_BK), BF16)]


def _gla_rows(q, k, g, v, chunk, get_state, put_state, o_ref, rsel_ref, a_ref, ds_ref, stb_ref):
    r = q.shape[0]
    ng = r // chunk
    row = lax.broadcasted_iota(jnp.int32, (r, WIDTH_BK), 0)
    pos = row % chunk
    b = g
    sh = 1
    while sh < chunk:
        b = b + jnp.where(pos >= sh, pltpu.roll(b, sh, axis=0), 0.0)
        sh *= 2
    b3 = b.reshape(ng, chunk, WIDTH_BK)
    q3 = q.reshape(ng, chunk, WIDTH_BK)
    k3 = k.reshape(ng, chunk, WIDTH_BK)
    bl3 = jnp.broadcast_to(b3[:, chunk - 1:chunk, :], b3.shape)
    bl = bl3.reshape(r, WIDTH_BK)

    gi = lax.broadcasted_iota(jnp.int32, (WIDTH_BK, WIDTH_BV), 0) // DK_B
    gj = lax.broadcasted_iota(jnp.int32, (WIDTH_BK, WIDTH_BV), 1) // DV_B
    expand = jnp.where(gi == gj, 1.0, 0.0).astype(BF16)
    di = lax.broadcasted_iota(jnp.int32, (WIDTH_BV, WIDTH_BK), 0) // DV_B
    dj = lax.broadcasted_iota(jnp.int32, (WIDTH_BV, WIDTH_BK), 1) // DK_B
    diag = di == dj

    nsub = GLA_SUB // chunk
    s_idx = lax.broadcasted_iota(jnp.int32, (nsub, chunk, WIDTH_BK), 1)
    n_sb = r // GLA_SUB
    ws = []
    for sb in range(n_sb):
        gs = slice(sb * nsub, (sb + 1) * nsub)
        for t in range(chunk):
            dec = jnp.exp(jnp.where(s_idx <= t, b3[gs, t:t + 1, :] - b3[gs], NEG_INF))
            a = dec * k3[gs] * q3[gs, t:t + 1, :]
            a_ref[sb, pl.ds(t * GLA_SUB, GLA_SUB), :] = a.reshape(GLA_SUB, WIDTH_BK).astype(BF16)
        ws.append(jnp.dot(a_ref[sb], expand, preferred_element_type=F32))
    o_intra = []
    for sb in range(n_sb):
        wv = ws[sb].reshape(chunk, GLA_SUB, WIDTH_BV) * v[sb * GLA_SUB:(sb + 1) * GLA_SUB][None]
        o_intra.append(jnp.dot(rsel_ref[...], wv.reshape(chunk * GLA_SUB, WIDTH_BV).astype(BF16),
                               preferred_element_type=F32))

    qd = (q * jnp.exp(b)).astype(BF16)
    kd = (k * jnp.exp(bl - b)).astype(BF16)
    gdec = jnp.exp(bl)
    vt = v.T.astype(BF16)
    grp = lax.broadcasted_iota(jnp.int32, (GLA_SUB, WIDTH_BK), 0) // chunk
    for j in range(ng):
        sb, jl = divmod(j, nsub)
        sub = slice(sb * GLA_SUB, (sb + 1) * GLA_SUB)
        kj = jnp.where(grp == jl, kd[sub], jnp.zeros((GLA_SUB, WIDTH_BK), BF16))
        ds = jnp.dot(vt[:, sub], kj, preferred_element_type=F32)
        ds_ref[j] = jnp.where(diag, ds, 0.0)
    for j in range(ng):
        st = get_state(j)
        stb_ref[j] = st.astype(BF16)
        put_state(j, st * gdec[j * chunk:j * chunk + 1, :] + ds_ref[j])
    for j in range(ng):
        rows = slice(j * chunk, (j + 1) * chunk)
        sb, jl = divmod(j, nsub)
        o_inter = lax.dot_general(qd[rows], stb_ref[j], NT_DIMS, preferred_element_type=F32)
        o_ref[rows, :] = o_intra[sb][jl * chunk:(jl + 1) * chunk] + o_inter


def _gla_rsel(rows, chunk):
    n = jnp.arange(rows)[:, None]
    c = jnp.arange(chunk * rows)[None, :]
    return ((c // rows == n % chunk) & ((c % rows) // chunk == n // chunk)).astype(BF16)


def _gla_prompt_body(q_ref, k_ref, g_ref, v_ref, rsel_ref, o_ref, s_ref, st_ref, *scratch, chunk):
    @pl.when(pl.program_id(1) == 0)
    def _():
        st_ref[...] = jnp.zeros(st_ref.shape, F32)

    def put(j, s):
        st_ref[...] = s

    _gla_rows(q_ref[0], k_ref[0], g_ref[0], v_ref[0], chunk, lambda j: st_ref[...], put, o_ref.at[0],
              rsel_ref, *scratch)
    s_ref[0] = st_ref[...]


def _gla_prompt(q, k, g, v, *, batch, seq, rows=512):
    chunk = math.gcd(seq, GLA_CHUNK)
    rows = min(rows, seq)
    v3 = lambda a: a.reshape(batch, seq, a.shape[-1])
    blk = lambda w: pl.BlockSpec((1, rows, w), lambda b, i: (b, i, 0))
    o, st = pl.pallas_call(
        functools.partial(_gla_prompt_body, chunk=chunk),
        grid=(batch, seq // rows),
        in_specs=[blk(WIDTH_BK), blk(WIDTH_BK), blk(WIDTH_BK), blk(WIDTH_BV),
                  _const_spec((GLA_SUB, chunk * GLA_SUB))],
        out_specs=[blk(WIDTH_BV), pl.BlockSpec((1, WIDTH_BV, WIDTH_BK), lambda b, i: (b, 0, 0))],
        out_shape=[jax.ShapeDtypeStruct((batch, seq, WIDTH_BV), F32),
                   jax.ShapeDtypeStruct((batch, WIDTH_BV, WIDTH_BK), F32)],
        scratch_shapes=[pltpu.VMEM((WIDTH_BV, WIDTH_BK), F32), *_gla_scratch(rows, chunk)],
        compiler_params=_cparams(("arbitrary", "arbitrary")),
        name="gla_prompt",
    )(v3(q), v3(k), v3(g), v3(v), _gla_rsel(GLA_SUB, chunk))
    return o.reshape(batch * seq, WIDTH_BV), st


def _gla_sample_body(q_ref, k_ref, g_ref, v_ref, s0_ref, rsel_ref, o_ref, s1_ref, *scratch, chunk):
    def put(j, s):
        s1_ref[j] = s

    _gla_rows(q_ref[...], k_ref[...], g_ref[...], v_ref[...], chunk, lambda j: s0_ref[j], put, o_ref,
              rsel_ref, *scratch)


def _gla_sample(q, k, g, v, st0, *, t_new, rows=128):
    t = q.shape[0]
    rows = min(rows, t)
    nb = rows // t_new
    blk = lambda w: pl.BlockSpec((rows, w), lambda i: (i, 0))
    sblk = pl.BlockSpec((nb, WIDTH_BV, WIDTH_BK), lambda i: (i, 0, 0))
    return pl.pallas_call(
        functools.partial(_gla_sample_body, chunk=t_new),
        grid=(t // rows,),
        in_specs=[blk(WIDTH_BK), blk(WIDTH_BK), blk(WIDTH_BK), blk(WIDTH_BV), sblk,
                  _const_spec((GLA_SUB, t_new * GLA_SUB))],
        out_specs=[blk(WIDTH_BV), sblk],
        out_shape=[jax.ShapeDtypeStruct((t, WIDTH_BV), F32), jax.ShapeDtypeStruct(st0.shape, F32)],
        scratch_shapes=_gla_scratch(rows, t_new),
        compiler_params=_cparams(("arbitrary",)),
        name="gla_sample",
    )(q, k, g, v, st0, _gla_rsel(GLA_SUB, t_new))


def _state_to_blockdiag(s):
    b = s.shape[0]
    eye = jnp.eye(N_HEADS_B, dtype=s.dtype)
    return jnp.einsum('bhkv,hg->bhvgk', s, eye).reshape(b, WIDTH_BV, WIDTH_BK)


def _state_from_blockdiag(st):
    b = st.shape[0]
    s5 = st.reshape(b, N_HEADS_B, DV_B, N_HEADS_B, DK_B)
    idx = jnp.arange(N_HEADS_B)
    return s5[:, idx, :, idx, :].transpose(1, 0, 3, 2)


def _outproj_body(y_ref, oa_ref, ob_ref, rb_ref, gg_ref, m64_ref, wa_ref, wb_ref, o_ref):
    ob = ob_ref[...]
    ms = jnp.dot((ob * ob).astype(BF16), m64_ref[...], preferred_element_type=F32)
    rb = rb_ref[...]
    gated = ob * lax.rsqrt(ms + EPS) * gg_ref[...] * (rb * jax.nn.sigmoid(rb))
    o_ref[...] = (y_ref[...]
                  + jnp.dot(oa_ref[...], wa_ref[...], preferred_element_type=F32)
                  + jnp.dot(gated.astype(BF16), wb_ref[...], preferred_element_type=F32))


def _outproj(y, oa, ob, rb, gg, m64, wa, wb, *, tm=512):
    t, d = y.shape
    tm = min(tm, t)
    row = lambda w: pl.BlockSpec((tm, w), lambda i: (i, 0))
    return pl.pallas_call(
        _outproj_body,
        grid=(t // tm,),
        in_specs=[row(d), row(WIDTH_A), row(WIDTH_BV), row(WIDTH_BV), _const_spec((1, WIDTH_BV)),
                  _const_spec((WIDTH_BV, WIDTH_BV)), _const_spec((WIDTH_A, d)), _const_spec((WIDTH_BV, d))],
        out_specs=row(d),
        out_shape=jax.ShapeDtypeStruct((t, d), F32),
        compiler_params=_cparams(("arbitrary",)),
        name="outproj",
    )(y, oa, ob, rb, gg, m64, wa, wb)


def _head_mean_matrix(width):
    i = jnp.arange(width) // HEAD_DIM
    return jnp.where(i[:, None] == i[None, :], 1.0 / HEAD_DIM, 0.0).astype(BF16)


def kernel(x_prompt, x_sample, cache_win_k, cache_win_v, state_gla, ffn1_norm, ffn1_w1, ffn1_w3, ffn1_w2,
           mix_norm, w_in, q_norm, k_norm, rel_bias, w_gk2, b_gk, gla_norm, w_out, ffn2_norm, ffn2_w1,
           ffn2_w3, ffn2_w2):
    batch, seq, d_model = x_prompt.shape
    dec_batch, dec_seq, _ = x_sample.shape
    depth = ffn1_w1.shape[0]
    w_buf = cache_win_k.shape[2]
    dilations = tuple(sorted((d for _, d in DILATED_BRANCHES), reverse=True))
    assert seq % (Q_BLOCK * dilations[0]) == 0
    assert all(w // d == Q_BLOCK for w, d in DILATED_BRANCHES)
    assert GLA_CHUNK % dec_seq == 0 and LANES % dec_seq == 0

    assert WIDTH_BV == MXU_N and 2 * WIDTH_BK == MXU_N and WIDTH_A % MXU_N == 0
    m64_a = _head_mean_matrix(MXU_N)
    m64_b = _head_mean_matrix(WIDTH_BV)
    tile2 = lambda g: jnp.tile(g, MXU_N // HEAD_DIM)[None, :]

    stage_bias = []
    for dil in dilations:
        i0, a0 = _prompt_bias_index(dil, Q_BLOCK, first=False)
        i1, a1 = _prompt_bias_index(dil, Q_BLOCK, first=True)
        tbl = _bias_table(rel_bias, jnp.concatenate([i0, i1], 0), jnp.concatenate([a0, a1], 0))
        stage_bias.append(tbl.reshape(N_HEADS_A, 2, Q_BLOCK, 2 * Q_BLOCK).transpose(1, 0, 2, 3))
    stage_bias = jnp.stack(stage_bias)
    qi = jnp.arange(dec_seq, dtype=jnp.int32)[:, None]
    sw_idx, sw_add = _sample_bias_index(w_buf - dec_seq + qi - jnp.arange(w_buf, dtype=jnp.int32)[None, :])
    sd_idx, sd_add = _sample_bias_index(w_buf + qi - jnp.arange(LANES, dtype=jnp.int32)[None, :])
    sd_add = jnp.where(jnp.arange(LANES)[None, :] < dec_seq, sd_add, NEG_INF)
    pair_rows = lambda t: t.reshape(N_PAIRS, 2 * dec_seq, t.shape[-1])
    bias_w = pair_rows(_bias_table(rel_bias, sw_idx, sw_add))
    bias_d = pair_rows(_bias_table(rel_bias, sd_idx, sd_add))

    yp = x_prompt.reshape(batch * seq, d_model)
    ys = x_sample.reshape(dec_batch * dec_seq, d_model)
    outs = [[] for _ in range(6)]
    for l in range(depth):
        bf = lambda w: w.astype(BF16)
        f1 = (ffn1_norm[l][None, :], bf(ffn1_w1[l]), bf(ffn1_w3[l]), bf(ffn1_w2[l]))
        f2 = (ffn2_norm[l][None, :], bf(ffn2_w1[l]), bf(ffn2_w3[l]), bf(ffn2_w2[l]))
        w_pad = jnp.pad(bf(w_in[l]), ((0, 0), (0, PROJ_PAD - w_in.shape[2])))
        wgk = jnp.pad(bf(w_gk2[l]), ((0, MXU_N - GATE_RANK), (0, 0)))
        pj = (mix_norm[l][None, :], w_pad, tile2(q_norm[l]), tile2(k_norm[l]), m64_a, wgk, b_gk[l][None, :])
        op = (jnp.tile(gla_norm[l], N_HEADS_B)[None, :], m64_b, bf(w_out[l][:WIDTH_A]), bf(w_out[l][WIDTH_A:]))

        ys1 = _ffn(ys, *f1)
        n_new = dec_batch * dec_seq
        qa_s, _, _, qb_s, kb_s, vb_s, rb_s, gk_s, kt_s, vt_s = _proj(ys1, *pj, seq=n_new, n_keep=n_new)
        lane_major = lambda c: c.transpose(0, 2, 3, 1).reshape(dec_batch, N_PAIRS, LANES, w_buf)
        ck, cv = lane_major(cache_win_k[l]), lane_major(cache_win_v[l])
        sp = _sample_specs(dec_seq, w_buf)
        win_shape = jax.ShapeDtypeStruct(ck.shape, F32)

        y1, wk, p_w, p_d = _ffn_side(
            yp, *f1, functools.partial(_sample_keys, t_new=dec_seq),
            (qa_s.reshape(dec_batch, dec_seq, WIDTH_A), kt_s[0], ck, bias_w, bias_d),
            (sp['q'], sp['new'], sp['win'], _const_spec(bias_w.shape), _const_spec(bias_d.shape)),
            (sp['win'], sp['pw'], sp['pd']),
            (win_shape, jax.ShapeDtypeStruct((dec_batch,) + bias_w.shape, BF16),
             jax.ShapeDtypeStruct((dec_batch,) + bias_d.shape, BF16)),
            steps=dec_batch, side_pieces=N_PAIRS * (SHIFT_PIECES + 1))
        n_keep = min(WIN_MAX, seq)
        qa, ka, va, qb, kb, vb, rb, gk, kt, vt = _proj(y1, *pj, seq=seq, n_keep=n_keep)
        oa = _attn_prompt(qa, ka, va, stage_bias, batch=batch, seq=seq, dilations=dilations)
        ob, st = _gla_prompt(qb, kb, gk, vb, batch=batch, seq=seq)
        yp, wv, oa_s = _ffn_side(
            _outproj(y1, oa, ob, rb, *op), *f2, functools.partial(_sample_values, t_new=dec_seq),
            (p_w, p_d, vt_s[0], cv), (sp['pw'], sp['pd'], sp['new'], sp['win']), (sp['win'], sp['q']),
            (win_shape, jax.ShapeDtypeStruct((dec_batch, dec_seq, WIDTH_A), BF16)),
            steps=dec_batch, side_pieces=N_PAIRS * (SHIFT_PIECES + 1))
        keep_major = lambda a: a.reshape(batch, N_HEADS_A, HEAD_DIM, n_keep).transpose(0, 3, 1, 2)
        outs[0].append(keep_major(kt))
        outs[1].append(keep_major(vt))
        outs[2].append(_state_from_blockdiag(st))

        ob, st = _gla_sample(qb_s, kb_s, gk_s, vb_s, _state_to_blockdiag(state_gla[l]), t_new=dec_seq)
        ys = _ffn(_outproj(ys1, oa_s.reshape(dec_batch * dec_seq, WIDTH_A), ob, rb_s, *op), *f2)
        row_major = lambda w: w.reshape(dec_batch, N_HEADS_A, HEAD_DIM, w_buf).transpose(0, 3, 1, 2)
        outs[3].append(row_major(wk))
        outs[4].append(row_major(wv))
        outs[5].append(_state_from_blockdiag(st))

    return (yp.reshape(batch, seq, d_model), ys.reshape(dec_batch, dec_seq, d_model),
            *(jnp.stack(o) for o in outs))
```

```python
import functools
import math

import jax
import jax.numpy as jnp
from jax import lax
from jax.experimental import pallas as pl
from jax.experimental.pallas import tpu as pltpu

F32 = jnp.float32
BF16 = jnp.bfloat16

HEAD_DIM = 64
N_HEADS_A = 12
N_HEADS_B = 4
DK_B = 32
DV_B = 64
GATE_RANK = 16
GATE_NORM = 16.0
GLA_CHUNK = 16
GLA_SUB = 128
DILATED_BRANCHES = ((128, 1), (512, 4), (2048, 16))
WIN_MAX = 2048
Q_BLOCK = 128
N_BUCKETS = 32
BUCKET_MAX_DIST = 2048
EPS = 1e-6
WIDTH_A = N_HEADS_A * HEAD_DIM
WIDTH_BK = N_HEADS_B * DK_B
WIDTH_BV = N_HEADS_B * DV_B

LANES = 128
MXU_N = 256
N_PAIRS = WIDTH_A // LANES
BLOCK_UNROLL = 16
VMEM_LIMIT = 56 * 1024 * 1024
NEG_INF = float("-inf")
LOG2_E = math.log2(math.e)
NT_DIMS = (((1,), (1,)), ((), ()))


def _cparams(sem):
    return pltpu.CompilerParams(dimension_semantics=sem, vmem_limit_bytes=VMEM_LIMIT)


def _const_spec(shape):
    nd = len(shape)
    return pl.BlockSpec(shape, lambda *_: (0,) * nd, pipeline_mode=pl.Buffered(1))


def _rms_rows(x, gain):
    return x * lax.rsqrt(jnp.mean(x * x, axis=-1, keepdims=True) + EPS) * gain


def _ffn_tile(x_ref, g_ref, w1_ref, w3_ref, w2_ref, o_ref, act_ref, fc):
    x = x_ref[...]
    h = _rms_rows(x, g_ref[...]).astype(BF16)
    for c in range(act_ref.shape[1] // fc):
        sl = pl.ds(c * fc, fc)
        a = jnp.dot(h, w1_ref[:, sl], preferred_element_type=F32)
        b = jnp.dot(h, w3_ref[:, sl], preferred_element_type=F32)
        act_ref[:, sl] = (a * jax.nn.sigmoid(a) * b).astype(BF16)
        yield
    act = act_ref[...]
    for c in range(o_ref.shape[1] // fc):
        sl = pl.ds(c * fc, fc)
        o_ref[:, sl] = x_ref[:, sl] + 0.5 * jnp.dot(act, w2_ref[:, sl], preferred_element_type=F32)
        yield


def _ffn_body(x_ref, g_ref, w1_ref, w3_ref, w2_ref, o_ref, act_ref, *, fc):
    for _ in _ffn_tile(x_ref, g_ref, w1_ref, w3_ref, w2_ref, o_ref, act_ref, fc):
        pass


def _ffn(x, gain, w1, w3, w2, *, tm=512, fc=256):
    t, d = x.shape
    f = w1.shape[1]
    tm = min(tm, t)
    return pl.pallas_call(
        functools.partial(_ffn_body, fc=fc),
        grid=(t // tm,),
        in_specs=[pl.BlockSpec((tm, d), lambda i: (i, 0)),
                  _const_spec((1, d)), _const_spec((d, f)), _const_spec((d, f)), _const_spec((f, d))],
        out_specs=pl.BlockSpec((tm, d), lambda i: (i, 0)),
        out_shape=jax.ShapeDtypeStruct((t, d), F32),
        scratch_shapes=[pltpu.VMEM((tm, f), BF16)],
        compiler_params=_cparams(("arbitrary",)),
        name="ffn",
    )(x, gain, w1, w3, w2)


def _ffn_side_body(*refs, fc, side, side_pieces, n_side_in, n_side_out):
    ffn_in, refs = refs[:5], refs[5:]
    side_in, refs = refs[:n_side_in], refs[n_side_in:]
    o_ref, side_out, act_ref = refs[0], refs[1:1 + n_side_out], refs[1 + n_side_out]
    side_gen = side(*side_in, *side_out, seq_id=pl.program_id(0))
    n_main = (act_ref.shape[1] + o_ref.shape[1]) // fc
    done = 0
    for i, _ in enumerate(_ffn_tile(*ffn_in, o_ref, act_ref, fc)):
        while done < ((i + 1) * side_pieces) // n_main:
            next(side_gen)
            done += 1
    assert done == side_pieces and next(side_gen, "exhausted") == "exhausted"


def _ffn_side(x, gain, w1, w3, w2, side, side_ins, side_in_specs, side_out_specs, side_out_shape,
              *, steps, side_pieces, fc=256):
    t, d = x.shape
    f = w1.shape[1]
    assert t % steps == 0
    tm = t // steps
    assert tm % 8 == 0
    row = pl.BlockSpec((tm, d), lambda i: (i, 0))
    return pl.pallas_call(
        functools.partial(_ffn_side_body, fc=fc, side=side, side_pieces=side_pieces,
                          n_side_in=len(side_ins), n_side_out=len(side_out_specs)),
        grid=(steps,),
        in_specs=[row, _const_spec((1, d)), _const_spec((d, f)), _const_spec((d, f)), _const_spec((f, d)),
                  *side_in_specs],
        out_specs=[row, *side_out_specs],
        out_shape=[jax.ShapeDtypeStruct((t, d), F32), *side_out_shape],
        scratch_shapes=[pltpu.VMEM((tm, f), BF16)],
        compiler_params=_cparams(("arbitrary",)),
        name="ffn_side",
    )(x, gain, w1, w3, w2, *side_ins)


def _sample_specs(t_new, w_buf):
    per_tile = LANES // t_new
    return dict(
        q=pl.BlockSpec((1, t_new, WIDTH_A), lambda i: (i, 0, 0)),
        new=pl.BlockSpec((WIDTH_A, LANES), lambda i: (0, i // per_tile)),
        win=pl.BlockSpec((1, N_PAIRS, LANES, w_buf), lambda i: (i, 0, 0, 0)),
        pw=pl.BlockSpec((1, N_PAIRS, 2 * t_new, w_buf), lambda i: (i, 0, 0, 0)),
        pd=pl.BlockSpec((1, N_PAIRS, 2 * t_new, LANES), lambda i: (i, 0, 0, 0)),
    )


_OFF_Q, _OFF_K, _OFF_V = 0, WIDTH_A, 2 * WIDTH_A
_OFF_QB = 3 * WIDTH_A
_OFF_KB = _OFF_QB + WIDTH_BK
_OFF_VB = _OFF_KB + WIDTH_BK
_OFF_RB = _OFF_VB + WIDTH_BV
_OFF_GL = _OFF_RB + WIDTH_BV
PROJ_PAD = _OFF_GL + MXU_N


def _proj_body(y_ref, g_ref, w_ref, qg_ref, kg_ref, m64_ref, wgk_ref, bgk_ref,
               qa_ref, ka_ref, va_ref, qb_ref, kb_ref, vb_ref, rb_ref, gk_ref, *kv_t_refs, keep):
    h = _rms_rows(y_ref[...], g_ref[...]).astype(BF16)
    if keep is not None:
        kt_ref, vt_ref = kv_t_refs
        in_keep = pl.program_id(0) % keep[0] >= keep[1]

    def cols(off):
        return jnp.dot(h, w_ref[:, pl.ds(off, MXU_N)], preferred_element_type=F32)

    groups = [pl.ds(gi * MXU_N, MXU_N) for gi in range(WIDTH_A // MXU_N)]
    for gi, sl in enumerate(groups):
        qa_ref[:, sl] = cols(_OFF_Q + gi * MXU_N)
        ka_ref[:, sl] = cols(_OFF_K + gi * MXU_N)
        va_ref[:, sl] = cols(_OFF_V + gi * MXU_N)
    qkb = cols(_OFF_QB)
    qb_ref[...] = qkb[:, 0:WIDTH_BK] * (DK_B ** -0.5)
    kb_ref[...] = qkb[:, WIDTH_BK:2 * WIDTH_BK]
    vb_ref[...] = cols(_OFF_VB)
    rb_ref[...] = cols(_OFF_RB)
    glr = cols(_OFF_GL).astype(BF16)

    m64 = m64_ref[...]
    for sl in groups:
        q = qa_ref[:, sl]
        ms = jnp.dot((q * q).astype(BF16), m64, preferred_element_type=F32)
        qa_ref[:, sl] = q * lax.rsqrt(ms + EPS) * qg_ref[...] * (HEAD_DIM ** -0.5 * LOG2_E)
        k = ka_ref[:, sl]
        ms = jnp.dot((k * k).astype(BF16), m64, preferred_element_type=F32)
        kn = k * lax.rsqrt(ms + EPS) * kg_ref[...]
        ka_ref[:, sl] = kn
        if keep is not None:
            @pl.when(in_keep)
            def _(kn=kn, sl=sl):
                kt_ref[0, sl, :] = kn.T
                vt_ref[0, sl, :] = va_ref[:, sl].T
    xg = jnp.dot(glr, wgk_ref[...], preferred_element_type=F32) + bgk_ref[...]
    gk_ref[...] = (jnp.minimum(xg, 0.0) - jnp.log(1.0 + jnp.exp(-jnp.abs(xg)))) * (1.0 / GATE_NORM)


def _proj(y, gain, w_pad, qg, kg, m64, wgk, bgk, *, tm=512, seq=None, n_keep=None):
    t, d = y.shape
    tm = min(tm, t)
    row = lambda w: pl.BlockSpec((tm, w), lambda i: (i, 0))
    widths = (WIDTH_A, WIDTH_A, WIDTH_A, WIDTH_BK, WIDTH_BK, WIDTH_BV, WIDTH_BV, WIDTH_BK)
    out_specs = [row(w) for w in widths]
    out_shape = [jax.ShapeDtypeStruct((t, w), F32) for w in widths]
    keep = None
    if n_keep is not None:
        assert seq % tm == 0 and n_keep % tm == 0
        tps = seq // tm
        keep = (tps, (seq - n_keep) // tm)
        tail = pl.BlockSpec((1, WIDTH_A, tm), lambda i: (i // tps, 0, jnp.maximum(i % tps - keep[1], 0)))
        out_specs += [tail, tail]
        out_shape += [jax.ShapeDtypeStruct((t // seq, WIDTH_A, n_keep), F32)] * 2
    return pl.pallas_call(
        functools.partial(_proj_body, keep=keep),
        grid=(t // tm,),
        in_specs=[row(d), _const_spec((1, d)), _const_spec(w_pad.shape),
                  _const_spec((1, MXU_N)), _const_spec((1, MXU_N)), _const_spec((MXU_N, MXU_N)),
                  _const_spec((MXU_N, WIDTH_BK)), _const_spec((1, WIDTH_BK))],
        out_specs=out_specs,
        out_shape=out_shape,
        compiler_params=_cparams(("arbitrary",)),
        name="proj",
    )(y, gain, w_pad, qg, kg, m64, wgk, bgk)


def _bias_body(rb_ref, *refs):
    n = len(refs) // 3
    for idx_ref, add_ref, o_ref in zip(refs[:n], refs[n:2 * n], refs[2 * n:]):
        idx = idx_ref[...]
        add = add_ref[...]
        for h in range(N_HEADS_A):
            acc = jnp.zeros(idx.shape, F32)
            for b in range(N_BUCKETS):
                acc = jnp.where(idx == b, rb_ref[b, h], acc)
            o_ref[h] = (acc + add) * LOG2_E


def _bias_tables(rel_bias, idx_add):
    idxs = [i for i, _ in idx_add]
    adds = [a for _, a in idx_add]
    spec = lambda a: pl.BlockSpec(a.shape, lambda: (0, 0))
    return pl.pallas_call(
        _bias_body,
        in_specs=[pl.BlockSpec(memory_space=pltpu.SMEM), *map(spec, idxs), *map(spec, adds)],
        out_specs=[pl.BlockSpec((N_HEADS_A,) + i.shape, lambda: (0, 0, 0)) for i in idxs],
        out_shape=[jax.ShapeDtypeStruct((N_HEADS_A,) + i.shape, F32) for i in idxs],
        name="bias_tables",
    )(rel_bias, *idxs, *adds)


def _bucket(dist):
    max_exact = N_BUCKETS // 2
    d = jnp.maximum(dist, 1).astype(F32)
    large = max_exact + (jnp.log(d / max_exact) / math.log(BUCKET_MAX_DIST / max_exact)
                         * (N_BUCKETS - max_exact)).astype(jnp.int32)
    large = jnp.minimum(large, N_BUCKETS - 1)
    return jnp.where(dist < max_exact, dist, large)


def _prompt_bias_index(dilation, nk, first):
    i = jnp.arange(Q_BLOCK, dtype=jnp.int32)[:, None]
    j = jnp.arange(Q_BLOCK + nk, dtype=jnp.int32)[None, :]
    step = i - j + nk
    valid = (step >= 0) & (step <= nk)
    if first:
        valid = valid & (j >= nk)
    idx = _bucket(jnp.clip(step, 0, nk) * dilation)
    return idx, jnp.where(valid, 0.0, NEG_INF).astype(F32)


def _sample_bias_index(delta):
    count = jnp.zeros(delta.shape, jnp.int32)
    for window, dil in DILATED_BRANCHES:
        count += ((delta >= 0) & (delta % dil == 0) & (delta <= window)).astype(jnp.int32)
    add = jnp.where(count > 0, jnp.log(jnp.maximum(count, 1).astype(F32)), NEG_INF)
    return _bucket(jnp.maximum(delta, 0)), add.astype(F32)


def _attn_prompt_body(q_ref, k_ref, v_ref, bias_ref, o_ref, qs, ks, vs, lses, outs, pf, *, dilations, base):
    seq = q_ref.shape[1]
    nblk = seq // Q_BLOCK
    lane = lax.broadcasted_iota(jnp.int32, (Q_BLOCK, LANES), 1)
    lo = lane < HEAD_DIM
    lane_row = lax.broadcasted_iota(jnp.int32, (1, LANES), 1)
    head_sel = (jnp.where(lane_row < HEAD_DIM, 1.0, 0.0).astype(BF16),
                jnp.where(lane_row < HEAD_DIM, 0.0, 1.0).astype(BF16))
    zeros_blk = jnp.zeros((Q_BLOCK, LANES), BF16)
    vs[:, LANES:] = jnp.ones((vs.shape[0], LANES), BF16)

    lb = seq // base
    for xi, x_ref in enumerate((q_ref, k_ref, v_ref)):
        for r in range(base):
            pf[xi, pl.ds(r * lb, lb), :] = x_ref[0, pl.ds(r, lb, stride=base), :]

    def subsequence(xi, x_ref, dil, r):
        ln = seq // dil
        if dil == 1:
            return x_ref[0]
        if dil == base:
            return pf[xi, pl.ds(r * lb, ln), :]
        assert dil % base == 0
        return pf[xi, pl.ds((r % base) * lb + r // base, ln, stride=dil // base), :]

    for bi, dil in enumerate(dilations):
        ln = seq // dil
        nqb = ln // Q_BLOCK
        kstride = ln + Q_BLOCK
        for r in range(dil):
            qs[pl.ds(r * ln, ln), :] = subsequence(0, q_ref, dil, r).astype(BF16)
            ks[pl.ds(r * kstride, Q_BLOCK), :] = zeros_blk
            vs[pl.ds(r * kstride, Q_BLOCK), 0:LANES] = zeros_blk
            ks[pl.ds(r * kstride + Q_BLOCK, ln), :] = subsequence(1, k_ref, dil, r).astype(BF16)
            vs[pl.ds(r * kstride + Q_BLOCK, ln), 0:LANES] = subsequence(2, v_ref, dil, r).astype(BF16)
        last = bi == len(dilations) - 1

        def block(ib, carry, bi=bi, dil=dil, ln=ln, nqb=nqb, kstride=kstride, last=last):
            r = ib // nqb
            qb = ib % nqb
            qrow = pl.multiple_of(r * ln + qb * Q_BLOCK, Q_BLOCK)
            krow = pl.multiple_of(r * kstride + qb * Q_BLOCK, Q_BLOCK)
            qblk = qs[pl.ds(qrow, Q_BLOCK), :]
            kblk = ks[pl.ds(krow, 2 * Q_BLOCK), :]
            vblk = vs[pl.ds(krow, 2 * Q_BLOCK), :]
            first_blk = jnp.asarray(qb == 0, jnp.int32)
            q2 = jnp.concatenate([qblk * head_sel[0], qblk * head_sel[1]], axis=0)
            s2 = lax.dot_general(q2, kblk, NT_DIMS, preferred_element_type=F32)
            ms_h, ps_h = [], []
            for hh in range(2):
                s = s2[hh * Q_BLOCK:(hh + 1) * Q_BLOCK] + bias_ref[bi, first_blk, hh]
                m = jnp.max(s, axis=1, keepdims=True)
                ms_h.append(m)
                ps_h.append(jnp.exp2(s - m).astype(BF16))
            pv = jnp.dot(jnp.concatenate(ps_h, axis=0), vblk, preferred_element_type=F32)
            m_p = jnp.where(lo, ms_h[0], ms_h[1])
            l_p = jnp.where(lo, pv[0:Q_BLOCK, LANES:], pv[Q_BLOCK:2 * Q_BLOCK, LANES:])
            acc = jnp.where(lo, pv[0:Q_BLOCK, 0:LANES], pv[Q_BLOCK:2 * Q_BLOCK, 0:LANES])
            if not last:
                dst = pl.ds(qb * (Q_BLOCK * dil) + r, Q_BLOCK, stride=dil)
                lses[bi, dst, :] = m_p + jnp.log2(l_p)
                outs[bi, dst, :] = acc * (1.0 / l_p)
            else:
                rows = pl.ds(qrow, Q_BLOCK)
                m_all = m_p
                for bj in range(len(dilations) - 1):
                    m_all = jnp.maximum(m_all, lses[bj, rows, :])
                w = jnp.exp2(m_p - m_all)
                den = w * l_p
                acc = w * acc
                for bj in range(len(dilations) - 1):
                    w = jnp.exp2(lses[bj, rows, :] - m_all)
                    den = den + w
                    acc = acc + w * outs[bj, rows, :]
                o_ref[0, rows, :] = (acc / den).astype(BF16)
            return carry

        lax.fori_loop(0, nblk, block, 0, unroll=BLOCK_UNROLL)


def _attn_prompt(q, k, v, bias, *, batch, seq, dilations):
    assert dilations[-1] == 1
    dmax = max(dilations)
    base = min(d for d in dilations if d > 1)
    assert all(d % base == 0 for d in dilations if d > 1)
    v3 = lambda a: a.reshape(batch, seq, WIDTH_A)
    blk = pl.BlockSpec((1, seq, LANES), lambda b, hp: (b, 0, hp))
    nb = len(dilations)
    out = pl.pallas_call(
        functools.partial(_attn_prompt_body, dilations=dilations, base=base),
        grid=(batch, N_PAIRS),
        in_specs=[blk, blk, blk,
                  pl.BlockSpec((nb, 2, 2, Q_BLOCK, 2 * Q_BLOCK), lambda b, hp: (0, 0, hp, 0, 0))],
        out_specs=blk,
        out_shape=jax.ShapeDtypeStruct((batch, seq, WIDTH_A), BF16),
        scratch_shapes=[pltpu.VMEM((seq, LANES), BF16),
                        pltpu.VMEM((seq + dmax * Q_BLOCK, LANES), BF16),
                        pltpu.VMEM((seq + dmax * Q_BLOCK, 2 * LANES), BF16),
                        pltpu.VMEM((nb - 1, seq, LANES), F32),
                        pltpu.VMEM((nb - 1, seq, LANES), F32),
                        pltpu.VMEM((3, seq, LANES), F32)],
        compiler_params=_cparams(("arbitrary", "arbitrary")),
        name="attn_prompt",
    )(v3(q), v3(k), v3(v), bias)
    return out.reshape(batch * seq, WIDTH_A)


SHIFT_PIECES = 2


def _shift_window(c_ref, n_ref, w_ref, hp, t_new, seq_id, result):
    w_buf = c_ref.shape[3]
    is_new = lax.broadcasted_iota(jnp.int32, (LANES, LANES), 1) >= LANES - t_new
    new_shift = (LANES - t_new) - t_new * (seq_id % (LANES // t_new))
    x = c_ref[0, hp]
    rolled = pltpu.roll(x, w_buf - t_new, axis=1)
    new = pltpu.roll(n_ref[pl.ds(hp * LANES, LANES), :], new_shift, axis=1)
    half = w_buf // 2
    w_ref[0, hp, :, 0:half] = rolled[:, 0:half]
    yield
    w_ref[0, hp, :, half:w_buf - LANES] = rolled[:, half:w_buf - LANES]
    w_ref[0, hp, :, w_buf - LANES:w_buf] = jnp.where(is_new, new, rolled[:, w_buf - LANES:w_buf])
    result.extend((w_ref[0, hp].astype(BF16), x[:, 0:LANES].astype(BF16)))


def _sample_keys(q_ref, kn_ref, ck_ref, bw_ref, bd_ref, wk_ref, pw_ref, pd_ref, *, t_new, seq_id):
    lo = lax.broadcasted_iota(jnp.int32, (t_new, LANES), 1) < HEAD_DIM
    for hp in range(ck_ref.shape[1]):
        q = q_ref[0, :, pl.ds(hp * LANES, LANES)]
        zero = jnp.zeros_like(q)
        q2 = jnp.concatenate([jnp.where(lo, q, zero), jnp.where(lo, zero, q)], axis=0).astype(BF16)
        shifted = []
        yield from _shift_window(ck_ref, kn_ref, wk_ref, hp, t_new, seq_id, shifted)
        kw, kd = shifted
        yield
        s_w = jnp.dot(q2, kw, preferred_element_type=F32) + bw_ref[hp]
        s_d = jnp.dot(q2, kd, preferred_element_type=F32) + bd_ref[hp]
        m = jnp.maximum(jnp.max(s_w, axis=1, keepdims=True), jnp.max(s_d, axis=1, keepdims=True))
        p_w = jnp.exp2(s_w - m)
        p_d = jnp.exp2(s_d - m)
        inv = 1.0 / (jnp.sum(p_w, axis=1, keepdims=True) + jnp.sum(p_d, axis=1, keepdims=True))
        pw_ref[0, hp] = (p_w * inv).astype(BF16)
        pd_ref[0, hp] = (p_d * inv).astype(BF16)
        yield


def _sample_values(pw_ref, pd_ref, vn_ref, cv_ref, wv_ref, o_ref, *, t_new, seq_id):
    lo = lax.broadcasted_iota(jnp.int32, (t_new, LANES), 1) < HEAD_DIM
    for hp in range(cv_ref.shape[1]):
        shifted = []
        yield from _shift_window(cv_ref, vn_ref, wv_ref, hp, t_new, seq_id, shifted)
        vw, vd = shifted
        yield
        o = (lax.dot_general(pw_ref[0, hp], vw, NT_DIMS, preferred_element_type=F32)
             + lax.dot_general(pd_ref[0, hp], vd, NT_DIMS, preferred_element_type=F32))
        o_ref[0, :, pl.ds(hp * LANES, LANES)] = jnp.where(lo, o[0:t_new], o[t_new:2 * t_new]).astype(BF16)
        yield


def _gla_scratch(rows, chunk):
    ng = rows // chunk
    return [pltpu.VMEM((rows // GLA_SUB, chunk * GLA_SUB, WIDTH_BK), BF16),
            pltpu.VMEM((ng, WIDTH_BV, WIDTH_BK), F32), pltpu.VMEM((ng, WIDTH_BV, WIDTH_BK), BF16)]


def _gla_rows(q, k, g, v, chunk, get_state, put_state, o_ref, rsel_ref, a_ref, ds_ref, stb_ref):
    r = q.shape[0]
    ng = r // chunk
    row = lax.broadcasted_iota(jnp.int32, (r, WIDTH_BK), 0)
    pos = row % chunk
    b = g
    sh = 1
    while sh < chunk:
        b = b + jnp.where(pos >= sh, pltpu.roll(b, sh, axis=0), 0.0)
        sh *= 2
    b3 = b.reshape(ng, chunk, WIDTH_BK)
    q3 = q.reshape(ng, chunk, WIDTH_BK)
    k3 = k.reshape(ng, chunk, WIDTH_BK)
    bl3 = jnp.broadcast_to(b3[:, chunk - 1:chunk, :], b3.shape)
    bl = bl3.reshape(r, WIDTH_BK)

    gi = lax.broadcasted_iota(jnp.int32, (WIDTH_BK, WIDTH_BV), 0) // DK_B
    gj = lax.broadcasted_iota(jnp.int32, (WIDTH_BK, WIDTH_BV), 1) // DV_B
    expand = jnp.where(gi == gj, 1.0, 0.0).astype(BF16)
    di = lax.broadcasted_iota(jnp.int32, (WIDTH_BV, WIDTH_BK), 0) // DV_B
    dj = lax.broadcasted_iota(jnp.int32, (WIDTH_BV, WIDTH_BK), 1) // DK_B
    diag = di == dj

    nsub = GLA_SUB // chunk
    s_idx = lax.broadcasted_iota(jnp.int32, (nsub, chunk, WIDTH_BK), 1)
    n_sb = r // GLA_SUB
    ws = []
    for sb in range(n_sb):
        gs = slice(sb * nsub, (sb + 1) * nsub)
        for t in range(chunk):
            dec = jnp.exp(jnp.where(s_idx <= t, b3[gs, t:t + 1, :] - b3[gs], NEG_INF))
            a = dec * k3[gs] * q3[gs, t:t + 1, :]
            a_ref[sb, pl.ds(t * GLA_SUB, GLA_SUB), :] = a.reshape(GLA_SUB, WIDTH_BK).astype(BF16)
        ws.append(jnp.dot(a_ref[sb], expand, preferred_element_type=F32))
    o_intra = []
    for sb in range(n_sb):
        wv = ws[sb].reshape(chunk, GLA_SUB, WIDTH_BV) * v[sb * GLA_SUB:(sb + 1) * GLA_SUB][None]
        o_intra.append(jnp.dot(rsel_ref[...], wv.reshape(chunk * GLA_SUB, WIDTH_BV).astype(BF16),
                               preferred_element_type=F32))

    qd = (q * jnp.exp(b)).astype(BF16)
    kd = (k * jnp.exp(bl - b)).astype(BF16)
    gdec = jnp.exp(bl)
    vt = v.T.astype(BF16)
    grp = lax.broadcasted_iota(jnp.int32, (GLA_SUB, WIDTH_BK), 0) // chunk
    for j in range(ng):
        sb, jl = divmod(j, nsub)
        sub = slice(sb * GLA_SUB, (sb + 1) * GLA_SUB)
        kj = jnp.where(grp == jl, kd[sub], jnp.zeros((GLA_SUB, WIDTH_BK), BF16))
        ds = jnp.dot(vt[:, sub], kj, preferred_element_type=F32)
        ds_ref[j] = jnp.where(diag, ds, 0.0)
    for j in range(ng):
        st = get_state(j)
        stb_ref[j] = st.astype(BF16)
        put_state(j, st * gdec[j * chunk:j * chunk + 1, :] + ds_ref[j])
    for j in range(ng):
        rows = slice(j * chunk, (j + 1) * chunk)
        sb, jl = divmod(j, nsub)
        o_inter = lax.dot_general(qd[rows], stb_ref[j], NT_DIMS, preferred_element_type=F32)
        o_ref[rows, :] = o_intra[sb][jl * chunk:(jl + 1) * chunk] + o_inter


def _gla_rsel(rows, chunk):
    n = jnp.arange(rows)[:, None]
    c = jnp.arange(chunk * rows)[None, :]
    return ((c // rows == n % chunk) & ((c % rows) // chunk == n // chunk)).astype(BF16)


def _gla_prompt_body(q_ref, k_ref, g_ref, v_ref, rsel_ref, o_ref, s_ref, st_ref, *scratch, chunk):
    @pl.when(pl.program_id(1) == 0)
    def _():
        st_ref[...] = jnp.zeros(st_ref.shape, F32)

    def put(j, s):
        st_ref[...] = s

    _gla_rows(q_ref[0], k_ref[0], g_ref[0], v_ref[0], chunk, lambda j: st_ref[...], put, o_ref.at[0],
              rsel_ref, *scratch)
    s_ref[0] = st_ref[...]


def _gla_prompt(q, k, g, v, *, batch, seq, rows=512):
    chunk = math.gcd(seq, GLA_CHUNK)
    rows = min(rows, seq)
    v3 = lambda a: a.reshape(batch, seq, a.shape[-1])
    blk = lambda w: pl.BlockSpec((1, rows, w), lambda b, i: (b, i, 0))
    o, st = pl.pallas_call(
        functools.partial(_gla_prompt_body, chunk=chunk),
        grid=(batch, seq // rows),
        in_specs=[blk(WIDTH_BK), blk(WIDTH_BK), blk(WIDTH_BK), blk(WIDTH_BV),
                  _const_spec((GLA_SUB, chunk * GLA_SUB))],
        out_specs=[blk(WIDTH_BV), pl.BlockSpec((1, WIDTH_BV, WIDTH_BK), lambda b, i: (b, 0, 0))],
        out_shape=[jax.ShapeDtypeStruct((batch, seq, WIDTH_BV), F32),
                   jax.ShapeDtypeStruct((batch, WIDTH_BV, WIDTH_BK), F32)],
        scratch_shapes=[pltpu.VMEM((WIDTH_BV, WIDTH_BK), F32), *_gla_scratch(rows, chunk)],
        compiler_params=_cparams(("arbitrary", "arbitrary")),
        name="gla_prompt",
    )(v3(q), v3(k), v3(g), v3(v), _gla_rsel(GLA_SUB, chunk))
    return o.reshape(batch * seq, WIDTH_BV), st


def _gla_sample_body(q_ref, k_ref, g_ref, v_ref, s0_ref, rsel_ref, o_ref, s1_ref, *scratch, chunk):
    def put(j, s):
        s1_ref[j] = s

    _gla_rows(q_ref[...], k_ref[...], g_ref[...], v_ref[...], chunk, lambda j: s0_ref[j], put, o_ref,
              rsel_ref, *scratch)


def _gla_sample(q, k, g, v, st0, *, t_new, rows=128):
    t = q.shape[0]
    rows = min(rows, t)
    nb = rows // t_new
    blk = lambda w: pl.BlockSpec((rows, w), lambda i: (i, 0))
    sblk = pl.BlockSpec((nb, WIDTH_BV, WIDTH_BK), lambda i: (i, 0, 0))
    return pl.pallas_call(
        functools.partial(_gla_sample_body, chunk=t_new),
        grid=(t // rows,),
        in_specs=[blk(WIDTH_BK), blk(WIDTH_BK), blk(WIDTH_BK), blk(WIDTH_BV), sblk,
                  _const_spec((GLA_SUB, t_new * GLA_SUB))],
        out_specs=[blk(WIDTH_BV), sblk],
        out_shape=[jax.ShapeDtypeStruct((t, WIDTH_BV), F32), jax.ShapeDtypeStruct(st0.shape, F32)],
        scratch_shapes=_gla_scratch(rows, t_new),
        compiler_params=_cparams(("arbitrary",)),
        name="gla_sample",
    )(q, k, g, v, st0, _gla_rsel(GLA_SUB, t_new))


def _state_to_blockdiag(s):
    b = s.shape[0]
    eye = jnp.eye(N_HEADS_B, dtype=s.dtype)
    return jnp.einsum('bhkv,hg->bhvgk', s, eye).reshape(b, WIDTH_BV, WIDTH_BK)


def _state_from_blockdiag(st):
    b = st.shape[0]
    s5 = st.reshape(b, N_HEADS_B, DV_B, N_HEADS_B, DK_B)
    idx = jnp.arange(N_HEADS_B)
    return s5[:, idx, :, idx, :].transpose(1, 0, 3, 2)


def _outproj_body(y_ref, oa_ref, ob_ref, rb_ref, gg_ref, m64_ref, wa_ref, wb_ref, o_ref):
    ob = ob_ref[...]
    ms = jnp.dot((ob * ob).astype(BF16), m64_ref[...], preferred_element_type=F32)
    rb = rb_ref[...]
    gated = ob * lax.rsqrt(ms + EPS) * gg_ref[...] * (rb * jax.nn.sigmoid(rb))
    o_ref[...] = (y_ref[...]
                  + jnp.dot(oa_ref[...], wa_ref[...], preferred_element_type=F32)
                  + jnp.dot(gated.astype(BF16), wb_ref[...], preferred_element_type=F32))


def _outproj(y, oa, ob, rb, gg, m64, wa, wb, *, tm=512):
    t, d = y.shape
    tm = min(tm, t)
    row = lambda w: pl.BlockSpec((tm, w), lambda i: (i, 0))
    return pl.pallas_call(
        _outproj_body,
        grid=(t // tm,),
        in_specs=[row(d), row(WIDTH_A), row(WIDTH_BV), row(WIDTH_BV), _const_spec((1, WIDTH_BV)),
                  _const_spec((WIDTH_BV, WIDTH_BV)), _const_spec((WIDTH_A, d)), _const_spec((WIDTH_BV, d))],
        out_specs=row(d),
        out_shape=jax.ShapeDtypeStruct((t, d), F32),
        compiler_params=_cparams(("arbitrary",)),
        name="outproj",
    )(y, oa, ob, rb, gg, m64, wa, wb)


def _head_mean_matrix(width):
    i = jnp.arange(width) // HEAD_DIM
    return jnp.where(i[:, None] == i[None, :], 1.0 / HEAD_DIM, 0.0).astype(BF16)


def kernel(x_prompt, x_sample, cache_win_k, cache_win_v, state_gla, ffn1_norm, ffn1_w1, ffn1_w3, ffn1_w2,
           mix_norm, w_in, q_norm, k_norm, rel_bias, w_gk2, b_gk, gla_norm, w_out, ffn2_norm, ffn2_w1,
           ffn2_w3, ffn2_w2):
    batch, seq, d_model = x_prompt.shape
    dec_batch, dec_seq, _ = x_sample.shape
    depth = ffn1_w1.shape[0]
    w_buf = cache_win_k.shape[2]
    dilations = tuple(sorted((d for _, d in DILATED_BRANCHES), reverse=True))
    assert seq % (Q_BLOCK * dilations[0]) == 0
    assert all(w // d == Q_BLOCK for w, d in DILATED_BRANCHES)
    assert GLA_CHUNK % dec_seq == 0 and LANES % dec_seq == 0

    assert WIDTH_BV == MXU_N and 2 * WIDTH_BK == MXU_N and WIDTH_A % MXU_N == 0
    m64_a = _head_mean_matrix(MXU_N)
    m64_b = _head_mean_matrix(WIDTH_BV)
    tile2 = lambda g: jnp.tile(g, MXU_N // HEAD_DIM)[None, :]

    idx_add = []
    for dil in dilations:
        i0, a0 = _prompt_bias_index(dil, Q_BLOCK, first=False)
        i1, a1 = _prompt_bias_index(dil, Q_BLOCK, first=True)
        idx_add.append((jnp.concatenate([i0, i1], 0), jnp.concatenate([a0, a1], 0)))
    qi = jnp.arange(dec_seq, dtype=jnp.int32)[:, None]
    idx_add.append(_sample_bias_index(w_buf - dec_seq + qi - jnp.arange(w_buf, dtype=jnp.int32)[None, :]))
    sd_idx, sd_add = _sample_bias_index(w_buf + qi - jnp.arange(LANES, dtype=jnp.int32)[None, :])
    idx_add.append((sd_idx, jnp.where(jnp.arange(LANES)[None, :] < dec_seq, sd_add, NEG_INF)))
    *stage_tbl, tbl_w, tbl_d = _bias_tables(rel_bias, idx_add)
    stage_bias = jnp.stack([t.reshape(N_HEADS_A, 2, Q_BLOCK, 2 * Q_BLOCK).transpose(1, 0, 2, 3)
                            for t in stage_tbl])
    pair_rows = lambda t: t.reshape(N_PAIRS, 2 * dec_seq, t.shape[-1])
    bias_w, bias_d = pair_rows(tbl_w), pair_rows(tbl_d)

    yp = x_prompt.reshape(batch * seq, d_model)
    ys = x_sample.reshape(dec_batch * dec_seq, d_model)
    outs = [[] for _ in range(6)]
    for l in range(depth):
        bf = lambda w: w.astype(BF16)
        f1 = (ffn1_norm[l][None, :], bf(ffn1_w1[l]), bf(ffn1_w3[l]), bf(ffn1_w2[l]))
        f2 = (ffn2_norm[l][None, :], bf(ffn2_w1[l]), bf(ffn2_w3[l]), bf(ffn2_w2[l]))
        w_pad = jnp.pad(bf(w_in[l]), ((0, 0), (0, PROJ_PAD - w_in.shape[2])))
        wgk = jnp.pad(bf(w_gk2[l]), ((0, MXU_N - GATE_RANK), (0, 0)))
        pj = (mix_norm[l][None, :], w_pad, tile2(q_norm[l]), tile2(k_norm[l]), m64_a, wgk, b_gk[l][None, :])
        op = (jnp.tile(gla_norm[l], N_HEADS_B)[None, :], m64_b, bf(w_out[l][:WIDTH_A]), bf(w_out[l][WIDTH_A:]))

        ys1 = _ffn(ys, *f1)
        n_new = dec_batch * dec_seq
        qa_s, _, _, qb_s, kb_s, vb_s, rb_s, gk_s, kt_s, vt_s = _proj(ys1, *pj, seq=n_new, n_keep=n_new)
        lane_major = lambda c: c.transpose(0, 2, 3, 1).reshape(dec_batch, N_PAIRS, LANES, w_buf)
        ck, cv = lane_major(cache_win_k[l]), lane_major(cache_win_v[l])
        sp = _sample_specs(dec_seq, w_buf)
        win_shape = jax.ShapeDtypeStruct(ck.shape, F32)

        y1, wk, p_w, p_d = _ffn_side(
            yp, *f1, functools.partial(_sample_keys, t_new=dec_seq),
            (qa_s.reshape(dec_batch, dec_seq, WIDTH_A), kt_s[0], ck, bias_w, bias_d),
            (sp['q'], sp['new'], sp['win'], _const_spec(bias_w.shape), _const_spec(bias_d.shape)),
            (sp['win'], sp['pw'], sp['pd']),
            (win_shape, jax.ShapeDtypeStruct((dec_batch,) + bias_w.shape, BF16),
             jax.ShapeDtypeStruct((dec_batch,) + bias_d.shape, BF16)),
            steps=dec_batch, side_pieces=N_PAIRS * (SHIFT_PIECES + 1))
        n_keep = min(WIN_MAX, seq)
        qa, ka, va, qb, kb, vb, rb, gk, kt, vt = _proj(y1, *pj, seq=seq, n_keep=n_keep)
        oa = _attn_prompt(qa, ka, va, stage_bias, batch=batch, seq=seq, dilations=dilations)
        ob, st = _gla_prompt(qb, kb, gk, vb, batch=batch, seq=seq)
        yp, wv, oa_s = _ffn_side(
            _outproj(y1, oa, ob, rb, *op), *f2, functools.partial(_sample_values, t_new=dec_seq),
            (p_w, p_d, vt_s[0], cv), (sp['pw'], sp['pd'], sp['new'], sp['win']), (sp['win'], sp['q']),
            (win_shape, jax.ShapeDtypeStruct((dec_batch, dec_seq, WIDTH_A), BF16)),
            steps=dec_batch, side_pieces=N_PAIRS * (SHIFT_PIECES + 1))
        keep_major = lambda a: a.reshape(batch, N_HEADS_A, HEAD_DIM, n_keep).transpose(0, 3, 1, 2)
        outs[0].append(keep_major(kt))
        outs[1].append(keep_major(vt))
        outs[2].append(_state_from_blockdiag(st))

        ob, st = _gla_sample(qb_s, kb_s, gk_s, vb_s, _state_to_blockdiag(state_gla[l]), t_new=dec_seq)
        ys = _ffn(_outproj(ys1, oa_s.reshape(dec_batch * dec_seq, WIDTH_A), ob, rb_s, *op), *f2)
        row_major = lambda w: w.reshape(dec_batch, N_HEADS_A, HEAD_DIM, w_buf).transpose(0, 3, 1, 2)
        outs[3].append(row_major(wk))
        outs[4].append(row_major(wv))
        outs[5].append(_state_from_blockdiag(st))

    return (yp.reshape(batch, seq, d_model), ys.reshape(dec_batch, dec_seq, d_model),
            *(jnp.stack(o) for o in outs))
```

```python
import functools
import math

import jax
import jax.numpy as jnp
from jax import lax
from jax.experimental import pallas as pl
from jax.experimental.pallas import tpu as pltpu

F32 = jnp.float32
BF16 = jnp.bfloat16

HEAD_DIM = 64
N_HEADS_A = 12
N_HEADS_B = 4
DK_B = 32
DV_B = 64
GATE_RANK = 16
GATE_NORM = 16.0
GLA_CHUNK = 16
GLA_SUB = 128
DILATED_BRANCHES = ((128, 1), (512, 4), (2048, 16))
WIN_MAX = 2048
Q_BLOCK = 128
N_BUCKETS = 32
BUCKET_MAX_DIST = 2048
EPS = 1e-6
WIDTH_A = N_HEADS_A * HEAD_DIM
WIDTH_BK = N_HEADS_B * DK_B
WIDTH_BV = N_HEADS_B * DV_B

LANES = 128
MXU_N = 256
N_PAIRS = WIDTH_A // LANES
VMEM_LIMIT = 56 * 1024 * 1024
NEG_INF = float("-inf")
LOG2_E = math.log2(math.e)
NT_DIMS = (((1,), (1,)), ((), ()))


def _cparams(sem):
    return pltpu.CompilerParams(dimension_semantics=sem, vmem_limit_bytes=VMEM_LIMIT)


def _const_spec(shape):
    nd = len(shape)
    return pl.BlockSpec(shape, lambda *_: (0,) * nd, pipeline_mode=pl.Buffered(1))


def _rms_rows(x, gain):
    return x * lax.rsqrt(jnp.mean(x * x, axis=-1, keepdims=True) + EPS) * gain


def _ffn_tile(x_ref, g_ref, w1_ref, w3_ref, w2_ref, o_ref, act_ref, fc):
    x = x_ref[...]
    h = _rms_rows(x, g_ref[...]).astype(BF16)
    for c in range(act_ref.shape[1] // fc):
        sl = pl.ds(c * fc, fc)
        a = jnp.dot(h, w1_ref[:, sl], preferred_element_type=F32)
        b = jnp.dot(h, w3_ref[:, sl], preferred_element_type=F32)
        act_ref[:, sl] = (a * jax.nn.sigmoid(a) * b).astype(BF16)
        yield
    act = act_ref[...]
    for c in range(o_ref.shape[1] // fc):
        sl = pl.ds(c * fc, fc)
        o_ref[:, sl] = x_ref[:, sl] + 0.5 * jnp.dot(act, w2_ref[:, sl], preferred_element_type=F32)
        yield


def _ffn_body(x_ref, g_ref, w1_ref, w3_ref, w2_ref, o_ref, act_ref, *, fc):
    for _ in _ffn_tile(x_ref, g_ref, w1_ref, w3_ref, w2_ref, o_ref, act_ref, fc):
        pass


def _ffn(x, gain, w1, w3, w2, *, tm=512, fc=256):
    t, d = x.shape
    f = w1.shape[1]
    tm = min(tm, t)
    return pl.pallas_call(
        functools.partial(_ffn_body, fc=fc),
        grid=(t // tm,),
        in_specs=[pl.BlockSpec((tm, d), lambda i: (i, 0)),
                  _const_spec((1, d)), _const_spec((d, f)), _const_spec((d, f)), _const_spec((f, d))],
        out_specs=pl.BlockSpec((tm, d), lambda i: (i, 0)),
        out_shape=jax.ShapeDtypeStruct((t, d), F32),
        scratch_shapes=[pltpu.VMEM((tm, f), BF16)],
        compiler_params=_cparams(("arbitrary",)),
        name="ffn",
    )(x, gain, w1, w3, w2)


def _ffn_side_body(*refs, fc, side, side_pieces, n_side_in, n_side_out):
    ffn_in, refs = refs[:5], refs[5:]
    side_in, refs = refs[:n_side_in], refs[n_side_in:]
    o_ref, side_out, act_ref = refs[0], refs[1:1 + n_side_out], refs[1 + n_side_out]
    side_gen = side(*side_in, *side_out, seq_id=pl.program_id(0))
    n_main = (act_ref.shape[1] + o_ref.shape[1]) // fc
    done = 0
    for i, _ in enumerate(_ffn_tile(*ffn_in, o_ref, act_ref, fc)):
        while done < ((i + 1) * side_pieces) // n_main:
            next(side_gen)
            done += 1
    assert done == side_pieces and next(side_gen, "exhausted") == "exhausted"


def _ffn_side(x, gain, w1, w3, w2, side, side_ins, side_in_specs, side_out_specs, side_out_shape,
              *, steps, side_pieces, fc=256):
    t, d = x.shape
    f = w1.shape[1]
    assert t % steps == 0
    tm = t // steps
    assert tm % 8 == 0
    row = pl.BlockSpec((tm, d), lambda i: (i, 0))
    return pl.pallas_call(
        functools.partial(_ffn_side_body, fc=fc, side=side, side_pieces=side_pieces,
                          n_side_in=len(side_ins), n_side_out=len(side_out_specs)),
        grid=(steps,),
        in_specs=[row, _const_spec((1, d)), _const_spec((d, f)), _const_spec((d, f)), _const_spec((f, d)),
                  *side_in_specs],
        out_specs=[row, *side_out_specs],
        out_shape=[jax.ShapeDtypeStruct((t, d), F32), *side_out_shape],
        scratch_shapes=[pltpu.VMEM((tm, f), BF16)],
        compiler_params=_cparams(("arbitrary",)),
        name="ffn_side",
    )(x, gain, w1, w3, w2, *side_ins)


def _sample_specs(t_new, w_buf):
    per_tile = LANES // t_new
    return dict(
        q=pl.BlockSpec((1, t_new, WIDTH_A), lambda i: (i, 0, 0)),
        new=pl.BlockSpec((WIDTH_A, LANES), lambda i: (0, i // per_tile)),
        win=pl.BlockSpec((1, N_PAIRS, LANES, w_buf), lambda i: (i, 0, 0, 0)),
        pw=pl.BlockSpec((1, N_PAIRS, 2 * t_new, w_buf), lambda i: (i, 0, 0, 0)),
        pd=pl.BlockSpec((1, N_PAIRS, 2 * t_new, LANES), lambda i: (i, 0, 0, 0)),
    )


_OFF_Q, _OFF_K, _OFF_V = 0, WIDTH_A, 2 * WIDTH_A
_OFF_QB = 3 * WIDTH_A
_OFF_KB = _OFF_QB + WIDTH_BK
_OFF_VB = _OFF_KB + WIDTH_BK
_OFF_RB = _OFF_VB + WIDTH_BV
_OFF_GL = _OFF_RB + WIDTH_BV
PROJ_PAD = _OFF_GL + MXU_N


def _proj_body(y_ref, g_ref, w_ref, qg_ref, kg_ref, m64_ref, wgk_ref, bgk_ref,
               qa_ref, ka_ref, va_ref, qb_ref, kb_ref, vb_ref, rb_ref, gk_ref, *kv_t_refs, keep):
    h = _rms_rows(y_ref[...], g_ref[...]).astype(BF16)
    if keep is not None:
        kt_ref, vt_ref = kv_t_refs
        in_keep = pl.program_id(0) % keep[0] >= keep[1]

    def cols(off):
        return jnp.dot(h, w_ref[:, pl.ds(off, MXU_N)], preferred_element_type=F32)

    groups = [pl.ds(gi * MXU_N, MXU_N) for gi in range(WIDTH_A // MXU_N)]
    for gi, sl in enumerate(groups):
        qa_ref[:, sl] = cols(_OFF_Q + gi * MXU_N)
        ka_ref[:, sl] = cols(_OFF_K + gi * MXU_N)
        va_ref[:, sl] = cols(_OFF_V + gi * MXU_N)
    qkb = cols(_OFF_QB)
    qb_ref[...] = qkb[:, 0:WIDTH_BK] * (DK_B ** -0.5)
    kb_ref[...] = qkb[:, WIDTH_BK:2 * WIDTH_BK]
    vb_ref[...] = cols(_OFF_VB)
    rb_ref[...] = cols(_OFF_RB)
    glr = cols(_OFF_GL).astype(BF16)

    m64 = m64_ref[...]
    for sl in groups:
        q = qa_ref[:, sl]
        ms = jnp.dot((q * q).astype(BF16), m64, preferred_element_type=F32)
        qa_ref[:, sl] = q * lax.rsqrt(ms + EPS) * qg_ref[...] * (HEAD_DIM ** -0.5 * LOG2_E)
        k = ka_ref[:, sl]
        ms = jnp.dot((k * k).astype(BF16), m64, preferred_element_type=F32)
        kn = k * lax.rsqrt(ms + EPS) * kg_ref[...]
        ka_ref[:, sl] = kn
        if keep is not None:
            @pl.when(in_keep)
            def _(kn=kn, sl=sl):
                kt_ref[0, sl, :] = kn.T
                vt_ref[0, sl, :] = va_ref[:, sl].T
    xg = jnp.dot(glr, wgk_ref[...], preferred_element_type=F32) + bgk_ref[...]
    gk_ref[...] = (jnp.minimum(xg, 0.0) - jnp.log(1.0 + jnp.exp(-jnp.abs(xg)))) * (1.0 / GATE_NORM)


def _proj(y, gain, w_pad, qg, kg, m64, wgk, bgk, *, tm=512, seq=None, n_keep=None):
    t, d = y.shape
    tm = min(tm, t)
    row = lambda w: pl.BlockSpec((tm, w), lambda i: (i, 0))
    widths = (WIDTH_A, WIDTH_A, WIDTH_A, WIDTH_BK, WIDTH_BK, WIDTH_BV, WIDTH_BV, WIDTH_BK)
    out_specs = [row(w) for w in widths]
    out_shape = [jax.ShapeDtypeStruct((t, w), F32) for w in widths]
    keep = None
    if n_keep is not None:
        assert seq % tm == 0 and n_keep % tm == 0
        tps = seq // tm
        keep = (tps, (seq - n_keep) // tm)
        tail = pl.BlockSpec((1, WIDTH_A, tm), lambda i: (i // tps, 0, jnp.maximum(i % tps - keep[1], 0)))
        out_specs += [tail, tail]
        out_shape += [jax.ShapeDtypeStruct((t // seq, WIDTH_A, n_keep), F32)] * 2
    return pl.pallas_call(
        functools.partial(_proj_body, keep=keep),
        grid=(t // tm,),
        in_specs=[row(d), _const_spec((1, d)), _const_spec(w_pad.shape),
                  _const_spec((1, MXU_N)), _const_spec((1, MXU_N)), _const_spec((MXU_N, MXU_N)),
                  _const_spec((MXU_N, WIDTH_BK)), _const_spec((1, WIDTH_BK))],
        out_specs=out_specs,
        out_shape=out_shape,
        compiler_params=_cparams(("arbitrary",)),
        name="proj",
    )(y, gain, w_pad, qg, kg, m64, wgk, bgk)


def _bias_body(rb_ref, *refs):
    n = len(refs) // 3
    for idx_ref, add_ref, o_ref in zip(refs[:n], refs[n:2 * n], refs[2 * n:]):
        idx = idx_ref[...]
        add = add_ref[...]
        for h in range(N_HEADS_A):
            acc = jnp.zeros(idx.shape, F32)
            for b in range(N_BUCKETS):
                acc = jnp.where(idx == b, rb_ref[b, h], acc)
            o_ref[h] = (acc + add) * LOG2_E


def _bias_tables(rel_bias, idx_add):
    idxs = [i for i, _ in idx_add]
    adds = [a for _, a in idx_add]
    spec = lambda a: pl.BlockSpec(a.shape, lambda: (0, 0))
    return pl.pallas_call(
        _bias_body,
        in_specs=[pl.BlockSpec(memory_space=pltpu.SMEM), *map(spec, idxs), *map(spec, adds)],
        out_specs=[pl.BlockSpec((N_HEADS_A,) + i.shape, lambda: (0, 0, 0)) for i in idxs],
        out_shape=[jax.ShapeDtypeStruct((N_HEADS_A,) + i.shape, F32) for i in idxs],
        name="bias_tables",
    )(rel_bias, *idxs, *adds)


def _bucket(dist):
    max_exact = N_BUCKETS // 2
    d = jnp.maximum(dist, 1).astype(F32)
    large = max_exact + (jnp.log(d / max_exact) / math.log(BUCKET_MAX_DIST / max_exact)
                         * (N_BUCKETS - max_exact)).astype(jnp.int32)
    large = jnp.minimum(large, N_BUCKETS - 1)
    return jnp.where(dist < max_exact, dist, large)


def _prompt_bias_index(dilation, nk, first):
    i = jnp.arange(Q_BLOCK, dtype=jnp.int32)[:, None]
    j = jnp.arange(Q_BLOCK + nk, dtype=jnp.int32)[None, :]
    step = i - j + nk
    valid = (step >= 0) & (step <= nk)
    if first:
        valid = valid & (j >= nk)
    idx = _bucket(jnp.clip(step, 0, nk) * dilation)
    return idx, jnp.where(valid, 0.0, NEG_INF).astype(F32)


def _sample_bias_index(delta):
    count = jnp.zeros(delta.shape, jnp.int32)
    for window, dil in DILATED_BRANCHES:
        count += ((delta >= 0) & (delta % dil == 0) & (delta <= window)).astype(jnp.int32)
    add = jnp.where(count > 0, jnp.log(jnp.maximum(count, 1).astype(F32)), NEG_INF)
    return _bucket(jnp.maximum(delta, 0)), add.astype(F32)


def _attn_prompt_body(one_ref, q_ref, k_ref, v_ref, bias_ref, o_ref, qs, ks, vs, lses, outs, pf,
                      *, dilations, base):
    seq = q_ref.shape[1]
    nblk = seq // Q_BLOCK
    lane = lax.broadcasted_iota(jnp.int32, (Q_BLOCK, LANES), 1)
    lo = lane < HEAD_DIM
    lane_row = lax.broadcasted_iota(jnp.int32, (1, LANES), 1)
    head_sel = (jnp.where(lane_row < HEAD_DIM, 1.0, 0.0).astype(BF16),
                jnp.where(lane_row < HEAD_DIM, 0.0, 1.0).astype(BF16))
    zeros_blk = jnp.zeros((Q_BLOCK, LANES), BF16)
    vs[:, LANES:] = jnp.ones((vs.shape[0], LANES), BF16)

    lb = seq // base
    for xi, x_ref in enumerate((q_ref, k_ref, v_ref)):
        for r in range(base):
            pf[xi, pl.ds(r * lb, lb), :] = x_ref[0, pl.ds(r, lb, stride=base), :]

    def subsequence(xi, x_ref, dil, r):
        ln = seq // dil
        if dil == 1:
            return x_ref[0]
        if dil == base:
            return pf[xi, pl.ds(r * lb, ln), :]
        assert dil % base == 0
        return pf[xi, pl.ds((r % base) * lb + r // base, ln, stride=dil // base), :]

    for bi, dil in enumerate(dilations):
        ln = seq // dil
        nqb = ln // Q_BLOCK
        kstride = ln + Q_BLOCK
        for r in range(dil):
            qs[pl.ds(r * ln, ln), :] = subsequence(0, q_ref, dil, r).astype(BF16)
            ks[pl.ds(r * kstride, Q_BLOCK), :] = zeros_blk
            vs[pl.ds(r * kstride, Q_BLOCK), 0:LANES] = zeros_blk
            ks[pl.ds(r * kstride + Q_BLOCK, ln), :] = subsequence(1, k_ref, dil, r).astype(BF16)
            vs[pl.ds(r * kstride + Q_BLOCK, ln), 0:LANES] = subsequence(2, v_ref, dil, r).astype(BF16)
        last = bi == len(dilations) - 1

        def block(ib, bi=bi, dil=dil, ln=ln, nqb=nqb, kstride=kstride, last=last):
            r = ib // nqb
            qb = ib % nqb
            qrow = r * ln + qb * Q_BLOCK
            krow = r * kstride + qb * Q_BLOCK
            qblk = qs[pl.ds(qrow, Q_BLOCK), :]
            kblk = ks[pl.ds(krow, 2 * Q_BLOCK), :]
            vblk = vs[pl.ds(krow, 2 * Q_BLOCK), :]
            first_blk = int(qb == 0)
            q2 = jnp.concatenate([qblk * head_sel[0], qblk * head_sel[1]], axis=0)
            s2 = lax.dot_general(q2, kblk, NT_DIMS, preferred_element_type=F32)
            ms_h, ps_h = [], []
            for hh in range(2):
                s = s2[hh * Q_BLOCK:(hh + 1) * Q_BLOCK] + bias_ref[bi, first_blk, hh]
                m = jnp.max(s, axis=1, keepdims=True)
                ms_h.append(m)
                ps_h.append(jnp.exp2(s - m).astype(BF16))
            pv = jnp.dot(jnp.concatenate(ps_h, axis=0), vblk, preferred_element_type=F32)
            m_p = jnp.where(lo, ms_h[0], ms_h[1])
            l_p = jnp.where(lo, pv[0:Q_BLOCK, LANES:], pv[Q_BLOCK:2 * Q_BLOCK, LANES:])
            acc = jnp.where(lo, pv[0:Q_BLOCK, 0:LANES], pv[Q_BLOCK:2 * Q_BLOCK, 0:LANES])
            if not last:
                dst = pl.ds(qb * (Q_BLOCK * dil) + r, Q_BLOCK, stride=dil)
                lses[bi, dst, :] = m_p + jnp.log2(l_p)
                outs[bi, dst, :] = acc * (1.0 / l_p)
            else:
                rows = pl.ds(qrow, Q_BLOCK)
                m_all = m_p
                for bj in range(len(dilations) - 1):
                    m_all = jnp.maximum(m_all, lses[bj, rows, :])
                w = jnp.exp2(m_p - m_all)
                den = w * l_p
                acc = w * acc
                for bj in range(len(dilations) - 1):
                    w = jnp.exp2(lses[bj, rows, :] - m_all)
                    den = den + w
                    acc = acc + w * outs[bj, rows, :]
                o_ref[0, rows, :] = (acc / den).astype(BF16)

        def stage(_, carry, block=block):
            for ib in range(nblk):
                block(ib)
            return carry

        lax.fori_loop(0, one_ref[0], stage, 0)


def _attn_prompt(q, k, v, bias, *, batch, seq, dilations):
    assert dilations[-1] == 1
    dmax = max(dilations)
    base = min(d for d in dilations if d > 1)
    assert all(d % base == 0 for d in dilations if d > 1)
    v3 = lambda a: a.reshape(batch, seq, WIDTH_A)
    blk = pl.BlockSpec((1, seq, LANES), lambda b, hp: (b, 0, hp))
    nb = len(dilations)
    out = pl.pallas_call(
        functools.partial(_attn_prompt_body, dilations=dilations, base=base),
        grid=(batch, N_PAIRS),
        in_specs=[pl.BlockSpec(memory_space=pltpu.SMEM), blk, blk, blk,
                  pl.BlockSpec((nb, 2, 2, Q_BLOCK, 2 * Q_BLOCK), lambda b, hp: (0, 0, hp, 0, 0))],
        out_specs=blk,
        out_shape=jax.ShapeDtypeStruct((batch, seq, WIDTH_A), BF16),
        scratch_shapes=[pltpu.VMEM((seq, LANES), BF16),
                        pltpu.VMEM((seq + dmax * Q_BLOCK, LANES), BF16),
                        pltpu.VMEM((seq + dmax * Q_BLOCK, 2 * LANES), BF16),
                        pltpu.VMEM((nb - 1, seq, LANES), F32),
                        pltpu.VMEM((nb - 1, seq, LANES), F32),
                        pltpu.VMEM((3, seq, LANES), F32)],
        compiler_params=_cparams(("arbitrary", "arbitrary")),
        name="attn_prompt",
    )(jnp.ones((1,), jnp.int32), v3(q), v3(k), v3(v), bias)
    return out.reshape(batch * seq, WIDTH_A)


SHIFT_PIECES = 2


def _shift_window(c_ref, n_ref, w_ref, hp, t_new, seq_id, result):
    w_buf = c_ref.shape[3]
    is_new = lax.broadcasted_iota(jnp.int32, (LANES, LANES), 1) >= LANES - t_new
    new_shift = (LANES - t_new) - t_new * (seq_id % (LANES // t_new))
    x = c_ref[0, hp]
    rolled = pltpu.roll(x, w_buf - t_new, axis=1)
    new = pltpu.roll(n_ref[pl.ds(hp * LANES, LANES), :], new_shift, axis=1)
    half = w_buf // 2
    w_ref[0, hp, :, 0:half] = rolled[:, 0:half]
    yield
    w_ref[0, hp, :, half:w_buf - LANES] = rolled[:, half:w_buf - LANES]
    w_ref[0, hp, :, w_buf - LANES:w_buf] = jnp.where(is_new, new, rolled[:, w_buf - LANES:w_buf])
    result.extend((w_ref[0, hp].astype(BF16), x[:, 0:LANES].astype(BF16)))


def _sample_keys(q_ref, kn_ref, ck_ref, bw_ref, bd_ref, wk_ref, pw_ref, pd_ref, *, t_new, seq_id):
    lo = lax.broadcasted_iota(jnp.int32, (t_new, LANES), 1) < HEAD_DIM
    for hp in range(ck_ref.shape[1]):
        q = q_ref[0, :, pl.ds(hp * LANES, LANES)]
        zero = jnp.zeros_like(q)
        q2 = jnp.concatenate([jnp.where(lo, q, zero), jnp.where(lo, zero, q)], axis=0).astype(BF16)
        shifted = []
        yield from _shift_window(ck_ref, kn_ref, wk_ref, hp, t_new, seq_id, shifted)
        kw, kd = shifted
        yield
        s_w = jnp.dot(q2, kw, preferred_element_type=F32) + bw_ref[hp]
        s_d = jnp.dot(q2, kd, preferred_element_type=F32) + bd_ref[hp]
        m = jnp.maximum(jnp.max(s_w, axis=1, keepdims=True), jnp.max(s_d, axis=1, keepdims=True))
        p_w = jnp.exp2(s_w - m)
        p_d = jnp.exp2(s_d - m)
        inv = 1.0 / (jnp.sum(p_w, axis=1, keepdims=True) + jnp.sum(p_d, axis=1, keepdims=True))
        pw_ref[0, hp] = (p_w * inv).astype(BF16)
        pd_ref[0, hp] = (p_d * inv).astype(BF16)
        yield


def _sample_values(pw_ref, pd_ref, vn_ref, cv_ref, wv_ref, o_ref, *, t_new, seq_id):
    lo = lax.broadcasted_iota(jnp.int32, (t_new, LANES), 1) < HEAD_DIM
    for hp in range(cv_ref.shape[1]):
        shifted = []
        yield from _shift_window(cv_ref, vn_ref, wv_ref, hp, t_new, seq_id, shifted)
        vw, vd = shifted
        yield
        o = (lax.dot_general(pw_ref[0, hp], vw, NT_DIMS, preferred_element_type=F32)
             + lax.dot_general(pd_ref[0, hp], vd, NT_DIMS, preferred_element_type=F32))
        o_ref[0, :, pl.ds(hp * LANES, LANES)] = jnp.where(lo, o[0:t_new], o[t_new:2 * t_new]).astype(BF16)
        yield


def _gla_scratch(rows, chunk):
    ng = rows // chunk
    return [pltpu.VMEM((rows // GLA_SUB, chunk * GLA_SUB, WIDTH_BK), BF16),
            pltpu.VMEM((ng, WIDTH_BV, WIDTH_BK), F32), pltpu.VMEM((ng, WIDTH_BV, WIDTH_BK), BF16)]


def _gla_rows(q, k, g, v, chunk, get_state, put_state, o_ref, rsel_ref, a_ref, ds_ref, stb_ref):
    r = q.shape[0]
    ng = r // chunk
    row = lax.broadcasted_iota(jnp.int32, (r, WIDTH_BK), 0)
    pos = row % chunk
    b = g
    sh = 1
    while sh < chunk:
        b = b + jnp.where(pos >= sh, pltpu.roll(b, sh, axis=0), 0.0)
        sh *= 2
    b3 = b.reshape(ng, chunk, WIDTH_BK)
    q3 = q.reshape(ng, chunk, WIDTH_BK)
    k3 = k.reshape(ng, chunk, WIDTH_BK)
    bl3 = jnp.broadcast_to(b3[:, chunk - 1:chunk, :], b3.shape)
    bl = bl3.reshape(r, WIDTH_BK)

    gi = lax.broadcasted_iota(jnp.int32, (WIDTH_BK, WIDTH_BV), 0) // DK_B
    gj = lax.broadcasted_iota(jnp.int32, (WIDTH_BK, WIDTH_BV), 1) // DV_B
    expand = jnp.where(gi == gj, 1.0, 0.0).astype(BF16)
    di = lax.broadcasted_iota(jnp.int32, (WIDTH_BV, WIDTH_BK), 0) // DV_B
    dj = lax.broadcasted_iota(jnp.int32, (WIDTH_BV, WIDTH_BK), 1) // DK_B
    diag = di == dj

    nsub = GLA_SUB // chunk
    s_idx = lax.broadcasted_iota(jnp.int32, (nsub, chunk, WIDTH_BK), 1)
    n_sb = r // GLA_SUB
    ws = []
    for sb in range(n_sb):
        gs = slice(sb * nsub, (sb + 1) * nsub)
        for t in range(chunk):
            dec = jnp.exp(jnp.where(s_idx <= t, b3[gs, t:t + 1, :] - b3[gs], NEG_INF))
            a = dec * k3[gs] * q3[gs, t:t + 1, :]
            a_ref[sb, pl.ds(t * GLA_SUB, GLA_SUB), :] = a.reshape(GLA_SUB, WIDTH_BK).astype(BF16)
        ws.append(jnp.dot(a_ref[sb], expand, preferred_element_type=F32))
    o_intra = []
    for sb in range(n_sb):
        wv = ws[sb].reshape(chunk, GLA_SUB, WIDTH_BV) * v[sb * GLA_SUB:(sb + 1) * GLA_SUB][None]
        o_intra.append(jnp.dot(rsel_ref[...], wv.reshape(chunk * GLA_SUB, WIDTH_BV).astype(BF16),
                               preferred_element_type=F32))

    qd = (q * jnp.exp(b)).astype(BF16)
    kd = (k * jnp.exp(bl - b)).astype(BF16)
    gdec = jnp.exp(bl)
    vt = v.T.astype(BF16)
    grp = lax.broadcasted_iota(jnp.int32, (GLA_SUB, WIDTH_BK), 0) // chunk
    for j in range(ng):
        sb, jl = divmod(j, nsub)
        sub = slice(sb * GLA_SUB, (sb + 1) * GLA_SUB)
        kj = jnp.where(grp == jl, kd[sub], jnp.zeros((GLA_SUB, WIDTH_BK), BF16))
        ds = jnp.dot(vt[:, sub], kj, preferred_element_type=F32)
        ds_ref[j] = jnp.where(diag, ds, 0.0)
    for j in range(ng):
        st = get_state(j)
        stb_ref[j] = st.astype(BF16)
        put_state(j, st * gdec[j * chunk:j * chunk + 1, :] + ds_ref[j])
    for j in range(ng):
        rows = slice(j * chunk, (j + 1) * chunk)
        sb, jl = divmod(j, nsub)
        o_inter = lax.dot_general(qd[rows], stb_ref[j], NT_DIMS, preferred_element_type=F32)
        o_ref[rows, :] = o_intra[sb][jl * chunk:(jl + 1) * chunk] + o_inter


def _gla_rsel(rows, chunk):
    n = jnp.arange(rows)[:, None]
    c = jnp.arange(chunk * rows)[None, :]
    return ((c // rows == n % chunk) & ((c % rows) // chunk == n // chunk)).astype(BF16)


def _gla_prompt_body(q_ref, k_ref, g_ref, v_ref, rsel_ref, o_ref, s_ref, st_ref, *scratch, chunk):
    @pl.when(pl.program_id(1) == 0)
    def _():
        st_ref[...] = jnp.zeros(st_ref.shape, F32)

    def put(j, s):
        st_ref[...] = s

    _gla_rows(q_ref[0], k_ref[0], g_ref[0], v_ref[0], chunk, lambda j: st_ref[...], put, o_ref.at[0],
              rsel_ref, *scratch)
    s_ref[0] = st_ref[...]


def _gla_prompt(q, k, g, v, *, batch, seq, rows=512):
    chunk = math.gcd(seq, GLA_CHUNK)
    rows = min(rows, seq)
    v3 = lambda a: a.reshape(batch, seq, a.shape[-1])
    blk = lambda w: pl.BlockSpec((1, rows, w), lambda b, i: (b, i, 0))
    o, st = pl.pallas_call(
        functools.partial(_gla_prompt_body, chunk=chunk),
        grid=(batch, seq // rows),
        in_specs=[blk(WIDTH_BK), blk(WIDTH_BK), blk(WIDTH_BK), blk(WIDTH_BV),
                  _const_spec((GLA_SUB, chunk * GLA_SUB))],
        out_specs=[blk(WIDTH_BV), pl.BlockSpec((1, WIDTH_BV, WIDTH_BK), lambda b, i: (b, 0, 0))],
        out_shape=[jax.ShapeDtypeStruct((batch, seq, WIDTH_BV), F32),
                   jax.ShapeDtypeStruct((batch, WIDTH_BV, WIDTH_BK), F32)],
        scratch_shapes=[pltpu.VMEM((WIDTH_BV, WIDTH_BK), F32), *_gla_scratch(rows, chunk)],
        compiler_params=_cparams(("arbitrary", "arbitrary")),
        name="gla_prompt",
    )(v3(q), v3(k), v3(g), v3(v), _gla_rsel(GLA_SUB, chunk))
    return o.reshape(batch * seq, WIDTH_BV), st


def _gla_sample_body(q_ref, k_ref, g_ref, v_ref, s0_ref, rsel_ref, o_ref, s1_ref, *scratch, chunk):
    def put(j, s):
        s1_ref[j] = s

    _gla_rows(q_ref[...], k_ref[...], g_ref[...], v_ref[...], chunk, lambda j: s0_ref[j], put, o_ref,
              rsel_ref, *scratch)


def _gla_sample(q, k, g, v, st0, *, t_new, rows=128):
    t = q.shape[0]
    rows = min(rows, t)
    nb = rows // t_new
    blk = lambda w: pl.BlockSpec((rows, w), lambda i: (i, 0))
    sblk = pl.BlockSpec((nb, WIDTH_BV, WIDTH_BK), lambda i: (i, 0, 0))
    return pl.pallas_call(
        functools.partial(_gla_sample_body, chunk=t_new),
        grid=(t // rows,),
        in_specs=[blk(WIDTH_BK), blk(WIDTH_BK), blk(WIDTH_BK), blk(WIDTH_BV), sblk,
                  _const_spec((GLA_SUB, t_new * GLA_SUB))],
        out_specs=[blk(WIDTH_BV), sblk],
        out_shape=[jax.ShapeDtypeStruct((t, WIDTH_BV), F32), jax.ShapeDtypeStruct(st0.shape, F32)],
        scratch_shapes=_gla_scratch(rows, t_new),
        compiler_params=_cparams(("arbitrary",)),
        name="gla_sample",
    )(q, k, g, v, st0, _gla_rsel(GLA_SUB, t_new))


def _state_to_blockdiag(s):
    b = s.shape[0]
    eye = jnp.eye(N_HEADS_B, dtype=s.dtype)
    return jnp.einsum('bhkv,hg->bhvgk', s, eye).reshape(b, WIDTH_BV, WIDTH_BK)


def _state_from_blockdiag(st):
    b = st.shape[0]
    s5 = st.reshape(b, N_HEADS_B, DV_B, N_HEADS_B, DK_B)
    idx = jnp.arange(N_HEADS_B)
    return s5[:, idx, :, idx, :].transpose(1, 0, 3, 2)


def _outproj_body(y_ref, oa_ref, ob_ref, rb_ref, gg_ref, m64_ref, wa_ref, wb_ref, o_ref):
    ob = ob_ref[...]
    ms = jnp.dot((ob * ob).astype(BF16), m64_ref[...], preferred_element_type=F32)
    rb = rb_ref[...]
    gated = ob * lax.rsqrt(ms + EPS) * gg_ref[...] * (rb * jax.nn.sigmoid(rb))
    o_ref[...] = (y_ref[...]
                  + jnp.dot(oa_ref[...], wa_ref[...], preferred_element_type=F32)
                  + jnp.dot(gated.astype(BF16), wb_ref[...], preferred_element_type=F32))


def _outproj(y, oa, ob, rb, gg, m64, wa, wb, *, tm=512):
    t, d = y.shape
    tm = min(tm, t)
    row = lambda w: pl.BlockSpec((tm, w), lambda i: (i, 0))
    return pl.pallas_call(
        _outproj_body,
        grid=(t // tm,),
        in_specs=[row(d), row(WIDTH_A), row(WIDTH_BV), row(WIDTH_BV), _const_spec((1, WIDTH_BV)),
                  _const_spec((WIDTH_BV, WIDTH_BV)), _const_spec((WIDTH_A, d)), _const_spec((WIDTH_BV, d))],
        out_specs=row(d),
        out_shape=jax.ShapeDtypeStruct((t, d), F32),
        compiler_params=_cparams(("arbitrary",)),
        name="outproj",
    )(y, oa, ob, rb, gg, m64, wa, wb)


def _head_mean_matrix(width):
    i = jnp.arange(width) // HEAD_DIM
    return jnp.where(i[:, None] == i[None, :], 1.0 / HEAD_DIM, 0.0).astype(BF16)


def kernel(x_prompt, x_sample, cache_win_k, cache_win_v, state_gla, ffn1_norm, ffn1_w1, ffn1_w3, ffn1_w2,
           mix_norm, w_in, q_norm, k_norm, rel_bias, w_gk2, b_gk, gla_norm, w_out, ffn2_norm, ffn2_w1,
           ffn2_w3, ffn2_w2):
    batch, seq, d_model = x_prompt.shape
    dec_batch, dec_seq, _ = x_sample.shape
    depth = ffn1_w1.shape[0]
    w_buf = cache_win_k.shape[2]
    dilations = tuple(sorted((d for _, d in DILATED_BRANCHES), reverse=True))
    assert seq % (Q_BLOCK * dilations[0]) == 0
    assert all(w // d == Q_BLOCK for w, d in DILATED_BRANCHES)
    assert GLA_CHUNK % dec_seq == 0 and LANES % dec_seq == 0

    assert WIDTH_BV == MXU_N and 2 * WIDTH_BK == MXU_N and WIDTH_A % MXU_N == 0
    m64_a = _head_mean_matrix(MXU_N)
    m64_b = _head_mean_matrix(WIDTH_BV)
    tile2 = lambda g: jnp.tile(g, MXU_N // HEAD_DIM)[None, :]

    idx_add = []
    for dil in dilations:
        i0, a0 = _prompt_bias_index(dil, Q_BLOCK, first=False)
        i1, a1 = _prompt_bias_index(dil, Q_BLOCK, first=True)
        idx_add.append((jnp.concatenate([i0, i1], 0), jnp.concatenate([a0, a1], 0)))
    qi = jnp.arange(dec_seq, dtype=jnp.int32)[:, None]
    idx_add.append(_sample_bias_index(w_buf - dec_seq + qi - jnp.arange(w_buf, dtype=jnp.int32)[None, :]))
    sd_idx, sd_add = _sample_bias_index(w_buf + qi - jnp.arange(LANES, dtype=jnp.int32)[None, :])
    idx_add.append((sd_idx, jnp.where(jnp.arange(LANES)[None, :] < dec_seq, sd_add, NEG_INF)))
    *stage_tbl, tbl_w, tbl_d = _bias_tables(rel_bias, idx_add)
    stage_bias = jnp.stack([t.reshape(N_HEADS_A, 2, Q_BLOCK, 2 * Q_BLOCK).transpose(1, 0, 2, 3)
                            for t in stage_tbl])
    pair_rows = lambda t: t.reshape(N_PAIRS, 2 * dec_seq, t.shape[-1])
    bias_w, bias_d = pair_rows(tbl_w), pair_rows(tbl_d)

    yp = x_prompt.reshape(batch * seq, d_model)
    ys = x_sample.reshape(dec_batch * dec_seq, d_model)
    outs = [[] for _ in range(6)]
    for l in range(depth):
        bf = lambda w: w.astype(BF16)
        f1 = (ffn1_norm[l][None, :], bf(ffn1_w1[l]), bf(ffn1_w3[l]), bf(ffn1_w2[l]))
        f2 = (ffn2_norm[l][None, :], bf(ffn2_w1[l]), bf(ffn2_w3[l]), bf(ffn2_w2[l]))
        w_pad = jnp.pad(bf(w_in[l]), ((0, 0), (0, PROJ_PAD - w_in.shape[2])))
        wgk = jnp.pad(bf(w_gk2[l]), ((0, MXU_N - GATE_RANK), (0, 0)))
        pj = (mix_norm[l][None, :], w_pad, tile2(q_norm[l]), tile2(k_norm[l]), m64_a, wgk, b_gk[l][None, :])
        op = (jnp.tile(gla_norm[l], N_HEADS_B)[None, :], m64_b, bf(w_out[l][:WIDTH_A]), bf(w_out[l][WIDTH_A:]))

        ys1 = _ffn(ys, *f1)
        n_new = dec_batch * dec_seq
        qa_s, _, _, qb_s, kb_s, vb_s, rb_s, gk_s, kt_s, vt_s = _proj(ys1, *pj, seq=n_new, n_keep=n_new)
        lane_major = lambda c: c.transpose(0, 2, 3, 1).reshape(dec_batch, N_PAIRS, LANES, w_buf)
        ck, cv = lane_major(cache_win_k[l]), lane_major(cache_win_v[l])
        sp = _sample_specs(dec_seq, w_buf)
        win_shape = jax.ShapeDtypeStruct(ck.shape, F32)

        y1, wk, p_w, p_d = _ffn_side(
            yp, *f1, functools.partial(_sample_keys, t_new=dec_seq),
            (qa_s.reshape(dec_batch, dec_seq, WIDTH_A), kt_s[0], ck, bias_w, bias_d),
            (sp['q'], sp['new'], sp['win'], _const_spec(bias_w.shape), _const_spec(bias_d.shape)),
            (sp['win'], sp['pw'], sp['pd']),
            (win_shape, jax.ShapeDtypeStruct((dec_batch,) + bias_w.shape, BF16),
             jax.ShapeDtypeStruct((dec_batch,) + bias_d.shape, BF16)),
            steps=dec_batch, side_pieces=N_PAIRS * (SHIFT_PIECES + 1))
        n_keep = min(WIN_MAX, seq)
        qa, ka, va, qb, kb, vb, rb, gk, kt, vt = _proj(y1, *pj, seq=seq, n_keep=n_keep)
        oa = _attn_prompt(qa, ka, va, stage_bias, batch=batch, seq=seq, dilations=dilations)
        ob, st = _gla_prompt(qb, kb, gk, vb, batch=batch, seq=seq)
        yp, wv, oa_s = _ffn_side(
            _outproj(y1, oa, ob, rb, *op), *f2, functools.partial(_sample_values, t_new=dec_seq),
            (p_w, p_d, vt_s[0], cv), (sp['pw'], sp['pd'], sp['new'], sp['win']), (sp['win'], sp['q']),
            (win_shape, jax.ShapeDtypeStruct((dec_batch, dec_seq, WIDTH_A), BF16)),
            steps=dec_batch, side_pieces=N_PAIRS * (SHIFT_PIECES + 1))
        keep_major = lambda a: a.reshape(batch, N_HEADS_A, HEAD_DIM, n_keep).transpose(0, 3, 1, 2)
        outs[0].append(keep_major(kt))
        outs[1].append(keep_major(vt))
        outs[2].append(_state_from_blockdiag(st))

        ob, st = _gla_sample(qb_s, kb_s, gk_s, vb_s, _state_to_blockdiag(state_gla[l]), t_new=dec_seq)
        ys = _ffn(_outproj(ys1, oa_s.reshape(dec_batch * dec_seq, WIDTH_A), ob, rb_s, *op), *f2)
        row_major = lambda w: w.reshape(dec_batch, N_HEADS_A, HEAD_DIM, w_buf).transpose(0, 3, 1, 2)
        outs[3].append(row_major(wk))
        outs[4].append(row_major(wv))
        outs[5].append(_state_from_blockdiag(st))

    return (yp.reshape(batch, seq, d_model), ys.reshape(dec_batch, dec_seq, d_model),
            *(jnp.stack(o) for o in outs))
```

```python
import functools
import math

import jax
import jax.numpy as jnp
from jax import lax
from jax.experimental import pallas as pl
from jax.experimental.pallas import tpu as pltpu

F32 = jnp.float32
BF16 = jnp.bfloat16

HEAD_DIM = 64
N_HEADS_A = 12
N_HEADS_B = 4
DK_B = 32
DV_B = 64
GATE_RANK = 16
GATE_NORM = 16.0
GLA_CHUNK = 16
GLA_SUB = 128
DILATED_BRANCHES = ((128, 1), (512, 4), (2048, 16))
WIN_MAX = 2048
Q_BLOCK = 128
N_BUCKETS = 32
BUCKET_MAX_DIST = 2048
EPS = 1e-6
WIDTH_A = N_HEADS_A * HEAD_DIM
WIDTH_BK = N_HEADS_B * DK_B
WIDTH_BV = N_HEADS_B * DV_B

LANES = 128
MXU_N = 256
N_PAIRS = WIDTH_A // LANES
VMEM_LIMIT = 56 * 1024 * 1024
NEG_INF = float("-inf")
LOG2_E = math.log2(math.e)
NT_DIMS = (((1,), (1,)), ((), ()))


def _cparams(sem):
    return pltpu.CompilerParams(dimension_semantics=sem, vmem_limit_bytes=VMEM_LIMIT)


def _const_spec(shape):
    nd = len(shape)
    return pl.BlockSpec(shape, lambda *_: (0,) * nd, pipeline_mode=pl.Buffered(1))


def _rms_rows(x, gain):
    return x * lax.rsqrt(jnp.mean(x * x, axis=-1, keepdims=True) + EPS) * gain


def _ffn_tile(x_ref, g_ref, w1_ref, w3_ref, w2_ref, o_ref, act_ref, fc):
    x = x_ref[...]
    h = _rms_rows(x, g_ref[...]).astype(BF16)
    for c in range(act_ref.shape[1] // fc):
        sl = pl.ds(c * fc, fc)
        a = jnp.dot(h, w1_ref[:, sl], preferred_element_type=F32)
        b = jnp.dot(h, w3_ref[:, sl], preferred_element_type=F32)
        act_ref[:, sl] = (a * jax.nn.sigmoid(a) * b).astype(BF16)
        yield
    act = act_ref[...]
    for c in range(o_ref.shape[1] // fc):
        sl = pl.ds(c * fc, fc)
        o_ref[:, sl] = x_ref[:, sl] + 0.5 * jnp.dot(act, w2_ref[:, sl], preferred_element_type=F32)
        yield


def _ffn_body(x_ref, g_ref, w1_ref, w3_ref, w2_ref, o_ref, act_ref, *, fc):
    for _ in _ffn_tile(x_ref, g_ref, w1_ref, w3_ref, w2_ref, o_ref, act_ref, fc):
        pass


def _ffn(x, gain, w1, w3, w2, *, tm=512, fc=256):
    t, d = x.shape
    f = w1.shape[1]
    tm = min(tm, t)
    return pl.pallas_call(
        functools.partial(_ffn_body, fc=fc),
        grid=(t // tm,),
        in_specs=[pl.BlockSpec((tm, d), lambda i: (i, 0)),
                  _const_spec((1, d)), _const_spec((d, f)), _const_spec((d, f)), _const_spec((f, d))],
        out_specs=pl.BlockSpec((tm, d), lambda i: (i, 0)),
        out_shape=jax.ShapeDtypeStruct((t, d), F32),
        scratch_shapes=[pltpu.VMEM((tm, f), BF16)],
        compiler_params=_cparams(("arbitrary",)),
        name="ffn",
    )(x, gain, w1, w3, w2)


def _ffn_side_body(*refs, fc, side, side_pieces, n_side_in, n_side_out):
    ffn_in, refs = refs[:5], refs[5:]
    side_in, refs = refs[:n_side_in], refs[n_side_in:]
    o_ref, side_out, act_ref = refs[0], refs[1:1 + n_side_out], refs[1 + n_side_out]
    side_gen = side(*side_in, *side_out, seq_id=pl.program_id(0))
    n_main = (act_ref.shape[1] + o_ref.shape[1]) // fc
    done = 0
    for i, _ in enumerate(_ffn_tile(*ffn_in, o_ref, act_ref, fc)):
        while done < ((i + 1) * side_pieces) // n_main:
            next(side_gen)
            done += 1
    assert done == side_pieces and next(side_gen, "exhausted") == "exhausted"


def _ffn_side(x, gain, w1, w3, w2, side, side_ins, side_in_specs, side_out_specs, side_out_shape,
              *, steps, side_pieces, fc=256):
    t, d = x.shape
    f = w1.shape[1]
    assert t % steps == 0
    tm = t // steps
    assert tm % 8 == 0
    row = pl.BlockSpec((tm, d), lambda i: (i, 0))
    return pl.pallas_call(
        functools.partial(_ffn_side_body, fc=fc, side=side, side_pieces=side_pieces,
                          n_side_in=len(side_ins), n_side_out=len(side_out_specs)),
        grid=(steps,),
        in_specs=[row, _const_spec((1, d)), _const_spec((d, f)), _const_spec((d, f)), _const_spec((f, d)),
                  *side_in_specs],
        out_specs=[row, *side_out_specs],
        out_shape=[jax.ShapeDtypeStruct((t, d), F32), *side_out_shape],
        scratch_shapes=[pltpu.VMEM((tm, f), BF16)],
        compiler_params=_cparams(("arbitrary",)),
        name="ffn_side",
    )(x, gain, w1, w3, w2, *side_ins)


def _sample_specs(t_new, w_buf):
    per_tile = LANES // t_new
    return dict(
        q=pl.BlockSpec((1, t_new, WIDTH_A), lambda i: (i, 0, 0)),
        new=pl.BlockSpec((WIDTH_A, LANES), lambda i: (0, i // per_tile)),
        win=pl.BlockSpec((1, N_PAIRS, LANES, w_buf), lambda i: (i, 0, 0, 0)),
        pw=pl.BlockSpec((1, N_PAIRS, 2 * t_new, w_buf), lambda i: (i, 0, 0, 0)),
        pd=pl.BlockSpec((1, N_PAIRS, 2 * t_new, LANES), lambda i: (i, 0, 0, 0)),
    )


_OFF_Q, _OFF_K, _OFF_V = 0, WIDTH_A, 2 * WIDTH_A
_OFF_QB = 3 * WIDTH_A
_OFF_KB = _OFF_QB + WIDTH_BK
_OFF_VB = _OFF_KB + WIDTH_BK
_OFF_RB = _OFF_VB + WIDTH_BV
_OFF_GL = _OFF_RB + WIDTH_BV
PROJ_PAD = _OFF_GL + MXU_N


def _proj_body(y_ref, g_ref, w_ref, qg_ref, kg_ref, m64_ref, wgk_ref, bgk_ref,
               qa_ref, ka_ref, va_ref, qb_ref, kb_ref, vb_ref, rb_ref, gk_ref, *kv_t_refs, keep):
    h = _rms_rows(y_ref[...], g_ref[...]).astype(BF16)
    if keep is not None:
        kt_ref, vt_ref = kv_t_refs
        in_keep = pl.program_id(0) % keep[0] >= keep[1]

    def cols(off):
        return jnp.dot(h, w_ref[:, pl.ds(off, MXU_N)], preferred_element_type=F32)

    groups = [pl.ds(gi * MXU_N, MXU_N) for gi in range(WIDTH_A // MXU_N)]
    for gi, sl in enumerate(groups):
        qa_ref[:, sl] = cols(_OFF_Q + gi * MXU_N)
        ka_ref[:, sl] = cols(_OFF_K + gi * MXU_N)
        va_ref[:, sl] = cols(_OFF_V + gi * MXU_N)
    qkb = cols(_OFF_QB)
    qb_ref[...] = qkb[:, 0:WIDTH_BK] * (DK_B ** -0.5)
    kb_ref[...] = qkb[:, WIDTH_BK:2 * WIDTH_BK]
    vb_ref[...] = cols(_OFF_VB)
    rb_ref[...] = cols(_OFF_RB)
    glr = cols(_OFF_GL).astype(BF16)

    m64 = m64_ref[...]
    for sl in groups:
        q = qa_ref[:, sl]
        ms = jnp.dot((q * q).astype(BF16), m64, preferred_element_type=F32)
        qa_ref[:, sl] = q * lax.rsqrt(ms + EPS) * qg_ref[...] * (HEAD_DIM ** -0.5 * LOG2_E)
        k = ka_ref[:, sl]
        ms = jnp.dot((k * k).astype(BF16), m64, preferred_element_type=F32)
        kn = k * lax.rsqrt(ms + EPS) * kg_ref[...]
        ka_ref[:, sl] = kn
        if keep is not None:
            @pl.when(in_keep)
            def _(kn=kn, sl=sl):
                kt_ref[0, sl, :] = kn.T
                vt_ref[0, sl, :] = va_ref[:, sl].T
    xg = jnp.dot(glr, wgk_ref[...], preferred_element_type=F32) + bgk_ref[...]
    gk_ref[...] = (jnp.minimum(xg, 0.0) - jnp.log(1.0 + jnp.exp(-jnp.abs(xg)))) * (1.0 / GATE_NORM)


def _proj(y, gain, w_pad, qg, kg, m64, wgk, bgk, *, tm=512, seq=None, n_keep=None):
    t, d = y.shape
    tm = min(tm, t)
    row = lambda w: pl.BlockSpec((tm, w), lambda i: (i, 0))
    widths = (WIDTH_A, WIDTH_A, WIDTH_A, WIDTH_BK, WIDTH_BK, WIDTH_BV, WIDTH_BV, WIDTH_BK)
    out_specs = [row(w) for w in widths]
    out_shape = [jax.ShapeDtypeStruct((t, w), F32) for w in widths]
    keep = None
    if n_keep is not None:
        assert seq % tm == 0 and n_keep % tm == 0
        tps = seq // tm
        keep = (tps, (seq - n_keep) // tm)
        tail = pl.BlockSpec((1, WIDTH_A, tm), lambda i: (i // tps, 0, jnp.maximum(i % tps - keep[1], 0)))
        out_specs += [tail, tail]
        out_shape += [jax.ShapeDtypeStruct((t // seq, WIDTH_A, n_keep), F32)] * 2
    return pl.pallas_call(
        functools.partial(_proj_body, keep=keep),
        grid=(t // tm,),
        in_specs=[row(d), _const_spec((1, d)), _const_spec(w_pad.shape),
                  _const_spec((1, MXU_N)), _const_spec((1, MXU_N)), _const_spec((MXU_N, MXU_N)),
                  _const_spec((MXU_N, WIDTH_BK)), _const_spec((1, WIDTH_BK))],
        out_specs=out_specs,
        out_shape=out_shape,
        compiler_params=_cparams(("arbitrary",)),
        name="proj",
    )(y, gain, w_pad, qg, kg, m64, wgk, bgk)


def _bias_body(rb_ref, *refs):
    n = len(refs) // 3
    for idx_ref, add_ref, o_ref in zip(refs[:n], refs[n:2 * n], refs[2 * n:]):
        idx = idx_ref[...]
        add = add_ref[...]
        for h in range(N_HEADS_A):
            acc = jnp.zeros(idx.shape, F32)
            for b in range(N_BUCKETS):
                acc = jnp.where(idx == b, rb_ref[b, h], acc)
            o_ref[h] = (acc + add) * LOG2_E


def _bias_tables(rel_bias, idx_add):
    idxs = [i for i, _ in idx_add]
    adds = [a for _, a in idx_add]
    spec = lambda a: pl.BlockSpec(a.shape, lambda: (0, 0))
    return pl.pallas_call(
        _bias_body,
        in_specs=[pl.BlockSpec(memory_space=pltpu.SMEM), *map(spec, idxs), *map(spec, adds)],
        out_specs=[pl.BlockSpec((N_HEADS_A,) + i.shape, lambda: (0, 0, 0)) for i in idxs],
        out_shape=[jax.ShapeDtypeStruct((N_HEADS_A,) + i.shape, F32) for i in idxs],
        name="bias_tables",
    )(rel_bias, *idxs, *adds)


def _bucket(dist):
    max_exact = N_BUCKETS // 2
    d = jnp.maximum(dist, 1).astype(F32)
    large = max_exact + (jnp.log(d / max_exact) / math.log(BUCKET_MAX_DIST / max_exact)
                         * (N_BUCKETS - max_exact)).astype(jnp.int32)
    large = jnp.minimum(large, N_BUCKETS - 1)
    return jnp.where(dist < max_exact, dist, large)


def _prompt_bias_index(dilation, nk, first):
    i = jnp.arange(Q_BLOCK, dtype=jnp.int32)[:, None]
    j = jnp.arange(Q_BLOCK + nk, dtype=jnp.int32)[None, :]
    step = i - j + nk
    valid = (step >= 0) & (step <= nk)
    if first:
        valid = valid & (j >= nk)
    idx = _bucket(jnp.clip(step, 0, nk) * dilation)
    return idx, jnp.where(valid, 0.0, NEG_INF).astype(F32)


def _sample_bias_index(delta):
    count = jnp.zeros(delta.shape, jnp.int32)
    for window, dil in DILATED_BRANCHES:
        count += ((delta >= 0) & (delta % dil == 0) & (delta <= window)).astype(jnp.int32)
    add = jnp.where(count > 0, jnp.log(jnp.maximum(count, 1).astype(F32)), NEG_INF)
    return _bucket(jnp.maximum(delta, 0)), add.astype(F32)


def _attn_prompt_body(one_ref, q_ref, k_ref, v_ref, bias_ref, o_ref, qs, ks, vs, lses, outs, pf,
                      *, dilations, base):
    seq = q_ref.shape[1]
    nblk = seq // Q_BLOCK
    lane = lax.broadcasted_iota(jnp.int32, (Q_BLOCK, LANES), 1)
    lo = lane < HEAD_DIM
    lane_row = lax.broadcasted_iota(jnp.int32, (1, LANES), 1)
    head_sel = (jnp.where(lane_row < HEAD_DIM, 1.0, 0.0).astype(BF16),
                jnp.where(lane_row < HEAD_DIM, 0.0, 1.0).astype(BF16))
    zeros_blk = jnp.zeros((Q_BLOCK, LANES), BF16)
    vs[:, LANES:] = jnp.ones((vs.shape[0], LANES), BF16)

    lb = seq // base
    for xi, x_ref in enumerate((q_ref, k_ref, v_ref)):
        for r in range(base):
            pf[xi, pl.ds(r * lb, lb), :] = x_ref[0, pl.ds(r, lb, stride=base), :]

    def subsequence(xi, x_ref, dil, r):
        ln = seq // dil
        if dil == 1:
            return x_ref[0]
        if dil == base:
            return pf[xi, pl.ds(r * lb, ln), :]
        assert dil % base == 0
        return pf[xi, pl.ds((r % base) * lb + r // base, ln, stride=dil // base), :]

    for bi, dil in enumerate(dilations):
        ln = seq // dil
        nqb = ln // Q_BLOCK
        kstride = ln + Q_BLOCK
        for r in range(dil):
            qs[pl.ds(r * ln, ln), :] = subsequence(0, q_ref, dil, r).astype(BF16)
            ks[pl.ds(r * kstride, Q_BLOCK), :] = zeros_blk
            vs[pl.ds(r * kstride, Q_BLOCK), 0:LANES] = zeros_blk
            ks[pl.ds(r * kstride + Q_BLOCK, ln), :] = subsequence(1, k_ref, dil, r).astype(BF16)
            vs[pl.ds(r * kstride + Q_BLOCK, ln), 0:LANES] = subsequence(2, v_ref, dil, r).astype(BF16)
        last = bi == len(dilations) - 1

        def block(ib, bi=bi, dil=dil, ln=ln, nqb=nqb, kstride=kstride, last=last):
            r = ib // nqb
            qb = ib % nqb
            qrow = r * ln + qb * Q_BLOCK
            krow = r * kstride + qb * Q_BLOCK
            qblk = qs[pl.ds(qrow, Q_BLOCK), :]
            kblk = ks[pl.ds(krow, 2 * Q_BLOCK), :]
            vblk = vs[pl.ds(krow, 2 * Q_BLOCK), :]
            first_blk = int(qb == 0)
            q2 = jnp.concatenate([qblk * head_sel[0], qblk * head_sel[1]], axis=0)
            s2 = lax.dot_general(q2, kblk, NT_DIMS, preferred_element_type=F32)
            ms_h, ps_h = [], []
            for hh in range(2):
                s = s2[hh * Q_BLOCK:(hh + 1) * Q_BLOCK] + bias_ref[bi, first_blk, hh]
                m = jnp.max(s, axis=1, keepdims=True)
                ms_h.append(m)
                ps_h.append(jnp.exp2(s - m).astype(BF16))
            pv = jnp.dot(jnp.concatenate(ps_h, axis=0), vblk, preferred_element_type=F32)
            m_p = jnp.where(lo, ms_h[0], ms_h[1])
            l_p = jnp.where(lo, pv[0:Q_BLOCK, LANES:], pv[Q_BLOCK:2 * Q_BLOCK, LANES:])
            acc = jnp.where(lo, pv[0:Q_BLOCK, 0:LANES], pv[Q_BLOCK:2 * Q_BLOCK, 0:LANES])
            if not last:
                dst = pl.ds(qb * (Q_BLOCK * dil) + r, Q_BLOCK, stride=dil)
                lses[bi, dst, :] = m_p + jnp.log2(l_p)
                outs[bi, dst, :] = acc * (1.0 / l_p)
            else:
                rows = pl.ds(qrow, Q_BLOCK)
                m_all = m_p
                for bj in range(len(dilations) - 1):
                    m_all = jnp.maximum(m_all, lses[bj, rows, :])
                w = jnp.exp2(m_p - m_all)
                den = w * l_p
                acc = w * acc
                for bj in range(len(dilations) - 1):
                    w = jnp.exp2(lses[bj, rows, :] - m_all)
                    den = den + w
                    acc = acc + w * outs[bj, rows, :]
                o_ref[0, rows, :] = (acc / den).astype(BF16)

        def stage(_, carry, block=block):
            for ib in range(nblk):
                block(ib)
            return carry

        lax.fori_loop(0, one_ref[0], stage, 0)


def _attn_prompt(q, k, v, bias, *, batch, seq, dilations):
    assert dilations[-1] == 1
    dmax = max(dilations)
    base = min(d for d in dilations if d > 1)
    assert all(d % base == 0 for d in dilations if d > 1)
    v3 = lambda a: a.reshape(batch, seq, WIDTH_A)
    blk = pl.BlockSpec((1, seq, LANES), lambda b, hp: (b, 0, hp))
    nb = len(dilations)
    out = pl.pallas_call(
        functools.partial(_attn_prompt_body, dilations=dilations, base=base),
        grid=(batch, N_PAIRS),
        in_specs=[pl.BlockSpec(memory_space=pltpu.SMEM), blk, blk, blk,
                  pl.BlockSpec((nb, 2, 2, Q_BLOCK, 2 * Q_BLOCK), lambda b, hp: (0, 0, hp, 0, 0))],
        out_specs=blk,
        out_shape=jax.ShapeDtypeStruct((batch, seq, WIDTH_A), BF16),
        scratch_shapes=[pltpu.VMEM((seq, LANES), BF16),
                        pltpu.VMEM((seq + dmax * Q_BLOCK, LANES), BF16),
                        pltpu.VMEM((seq + dmax * Q_BLOCK, 2 * LANES), BF16),
                        pltpu.VMEM((nb - 1, seq, LANES), F32),
                        pltpu.VMEM((nb - 1, seq, LANES), F32),
                        pltpu.VMEM((3, seq, LANES), F32)],
        compiler_params=_cparams(("arbitrary", "arbitrary")),
        name="attn_prompt",
    )(jnp.ones((1,), jnp.int32), v3(q), v3(k), v3(v), bias)
    return out.reshape(batch * seq, WIDTH_A)


SHIFT_PIECES = 2


def _shift_window(c_ref, n_ref, w_ref, hp, t_new, seq_id, result):
    w_buf = c_ref.shape[3]
    is_new = lax.broadcasted_iota(jnp.int32, (LANES, LANES), 1) >= LANES - t_new
    new_shift = (LANES - t_new) - t_new * (seq_id % (LANES // t_new))
    x = c_ref[0, hp]
    rolled = pltpu.roll(x, w_buf - t_new, axis=1)
    new = pltpu.roll(n_ref[pl.ds(hp * LANES, LANES), :], new_shift, axis=1)
    half = w_buf // 2
    w_ref[0, hp, :, 0:half] = rolled[:, 0:half]
    yield
    w_ref[0, hp, :, half:w_buf - LANES] = rolled[:, half:w_buf - LANES]
    w_ref[0, hp, :, w_buf - LANES:w_buf] = jnp.where(is_new, new, rolled[:, w_buf - LANES:w_buf])
    result.extend((w_ref[0, hp].astype(BF16), x[:, 0:LANES].astype(BF16)))


def _sample_keys(q_ref, kn_ref, ck_ref, bw_ref, bd_ref, wk_ref, pw_ref, pd_ref, *, t_new, seq_id):
    lo = lax.broadcasted_iota(jnp.int32, (t_new, LANES), 1) < HEAD_DIM
    for hp in range(ck_ref.shape[1]):
        q = q_ref[0, :, pl.ds(hp * LANES, LANES)]
        zero = jnp.zeros_like(q)
        q2 = jnp.concatenate([jnp.where(lo, q, zero), jnp.where(lo, zero, q)], axis=0).astype(BF16)
        shifted = []
        yield from _shift_window(ck_ref, kn_ref, wk_ref, hp, t_new, seq_id, shifted)
        kw, kd = shifted
        yield
        s_w = jnp.dot(q2, kw, preferred_element_type=F32) + bw_ref[hp]
        s_d = jnp.dot(q2, kd, preferred_element_type=F32) + bd_ref[hp]
        m = jnp.maximum(jnp.max(s_w, axis=1, keepdims=True), jnp.max(s_d, axis=1, keepdims=True))
        p_w = jnp.exp2(s_w - m)
        p_d = jnp.exp2(s_d - m)
        inv = 1.0 / (jnp.sum(p_w, axis=1, keepdims=True) + jnp.sum(p_d, axis=1, keepdims=True))
        pw_ref[0, hp] = (p_w * inv).astype(BF16)
        pd_ref[0, hp] = (p_d * inv).astype(BF16)
        yield


def _sample_values(pw_ref, pd_ref, vn_ref, cv_ref, wv_ref, o_ref, *, t_new, seq_id):
    lo = lax.broadcasted_iota(jnp.int32, (t_new, LANES), 1) < HEAD_DIM
    for hp in range(cv_ref.shape[1]):
        shifted = []
        yield from _shift_window(cv_ref, vn_ref, wv_ref, hp, t_new, seq_id, shifted)
        vw, vd = shifted
        yield
        o = (lax.dot_general(pw_ref[0, hp], vw, NT_DIMS, preferred_element_type=F32)
             + lax.dot_general(pd_ref[0, hp], vd, NT_DIMS, preferred_element_type=F32))
        o_ref[0, :, pl.ds(hp * LANES, LANES)] = jnp.where(lo, o[0:t_new], o[t_new:2 * t_new]).astype(BF16)
        yield


def _gla_scratch(rows, chunk):
    ng = rows // chunk
    return [pltpu.VMEM((rows // GLA_SUB, chunk * GLA_SUB, WIDTH_BK), BF16),
            pltpu.VMEM((ng, WIDTH_BV, WIDTH_BK), F32), pltpu.VMEM((ng, WIDTH_BV, WIDTH_BK), BF16)]


def _gla_rows(q, k, g, v, chunk, get_state, put_state, o_ref, rsel_ref, a_ref, ds_ref, stb_ref):
    r = q.shape[0]
    ng = r // chunk
    row = lax.broadcasted_iota(jnp.int32, (r, WIDTH_BK), 0)
    pos = row % chunk
    b = g
    sh = 1
    while sh < chunk:
        b = b + jnp.where(pos >= sh, pltpu.roll(b, sh, axis=0), 0.0)
        sh *= 2
    b3 = b.reshape(ng, chunk, WIDTH_BK)
    q3 = q.reshape(ng, chunk, WIDTH_BK)
    k3 = k.reshape(ng, chunk, WIDTH_BK)
    bl3 = jnp.broadcast_to(b3[:, chunk - 1:chunk, :], b3.shape)
    bl = bl3.reshape(r, WIDTH_BK)

    gi = lax.broadcasted_iota(jnp.int32, (WIDTH_BK, WIDTH_BV), 0) // DK_B
    gj = lax.broadcasted_iota(jnp.int32, (WIDTH_BK, WIDTH_BV), 1) // DV_B
    expand = jnp.where(gi == gj, 1.0, 0.0).astype(BF16)
    di = lax.broadcasted_iota(jnp.int32, (WIDTH_BV, WIDTH_BK), 0) // DV_B
    dj = lax.broadcasted_iota(jnp.int32, (WIDTH_BV, WIDTH_BK), 1) // DK_B
    diag = di == dj

    nsub = GLA_SUB // chunk
    s_idx = lax.broadcasted_iota(jnp.int32, (nsub, chunk, WIDTH_BK), 1)
    n_sb = r // GLA_SUB
    ws = []
    for sb in range(n_sb):
        gs = slice(sb * nsub, (sb + 1) * nsub)
        for t in range(chunk):
            dec = jnp.exp(jnp.where(s_idx <= t, b3[gs, t:t + 1, :] - b3[gs], NEG_INF))
            a = dec * k3[gs] * q3[gs, t:t + 1, :]
            a_ref[sb, pl.ds(t * GLA_SUB, GLA_SUB), :] = a.reshape(GLA_SUB, WIDTH_BK).astype(BF16)
        ws.append(jnp.dot(a_ref[sb], expand, preferred_element_type=F32))
    o_intra = []
    for sb in range(n_sb):
        wv = ws[sb].reshape(chunk, GLA_SUB, WIDTH_BV) * v[sb * GLA_SUB:(sb + 1) * GLA_SUB][None]
        o_intra.append(jnp.dot(rsel_ref[...], wv.reshape(chunk * GLA_SUB, WIDTH_BV).astype(BF16),
                               preferred_element_type=F32))

    qd = (q * jnp.exp(b)).astype(BF16)
    kd = (k * jnp.exp(bl - b)).astype(BF16)
    gdec = jnp.exp(bl)
    vt = v.T.astype(BF16)
    grp = lax.broadcasted_iota(jnp.int32, (GLA_SUB, WIDTH_BK), 0) // chunk
    for j in range(ng):
        sb, jl = divmod(j, nsub)
        sub = slice(sb * GLA_SUB, (sb + 1) * GLA_SUB)
        kj = jnp.where(grp == jl, kd[sub], jnp.zeros((GLA_SUB, WIDTH_BK), BF16))
        ds = jnp.dot(vt[:, sub], kj, preferred_element_type=F32)
        ds_ref[j] = jnp.where(diag, ds, 0.0)
    for j in range(ng):
        st = get_state(j)
        stb_ref[j] = st.astype(BF16)
        put_state(j, st * gdec[j * chunk:j * chunk + 1, :] + ds_ref[j])
    for j in range(ng):
        rows = slice(j * chunk, (j + 1) * chunk)
        sb, jl = divmod(j, nsub)
        o_inter = lax.dot_general(qd[rows], stb_ref[j], NT_DIMS, preferred_element_type=F32)
        o_ref[rows, :] = o_intra[sb][jl * chunk:(jl + 1) * chunk] + o_inter


def _gla_rsel(rows, chunk):
    n = jnp.arange(rows)[:, None]
    c = jnp.arange(chunk * rows)[None, :]
    return ((c // rows == n % chunk) & ((c % rows) // chunk == n // chunk)).astype(BF16)


def _gla_prompt_body(q_ref, k_ref, g_ref, v_ref, rsel_ref, o_ref, s_ref, st_ref, *scratch, chunk):
    @pl.when(pl.program_id(1) == 0)
    def _():
        st_ref[...] = jnp.zeros(st_ref.shape, F32)

    def put(j, s):
        st_ref[...] = s

    _gla_rows(q_ref[0], k_ref[0], g_ref[0], v_ref[0], chunk, lambda j: st_ref[...], put, o_ref.at[0],
              rsel_ref, *scratch)
    s_ref[0] = st_ref[...]


def _gla_prompt(q, k, g, v, *, batch, seq, rows=1024):
    chunk = math.gcd(seq, GLA_CHUNK)
    rows = min(rows, seq)
    v3 = lambda a: a.reshape(batch, seq, a.shape[-1])
    blk = lambda w: pl.BlockSpec((1, rows, w), lambda b, i: (b, i, 0))
    o, st = pl.pallas_call(
        functools.partial(_gla_prompt_body, chunk=chunk),
        grid=(batch, seq // rows),
        in_specs=[blk(WIDTH_BK), blk(WIDTH_BK), blk(WIDTH_BK), blk(WIDTH_BV),
                  _const_spec((GLA_SUB, chunk * GLA_SUB))],
        out_specs=[blk(WIDTH_BV), pl.BlockSpec((1, WIDTH_BV, WIDTH_BK), lambda b, i: (b, 0, 0))],
        out_shape=[jax.ShapeDtypeStruct((batch, seq, WIDTH_BV), F32),
                   jax.ShapeDtypeStruct((batch, WIDTH_BV, WIDTH_BK), F32)],
        scratch_shapes=[pltpu.VMEM((WIDTH_BV, WIDTH_BK), F32), *_gla_scratch(rows, chunk)],
        compiler_params=_cparams(("arbitrary", "arbitrary")),
        name="gla_prompt",
    )(v3(q), v3(k), v3(g), v3(v), _gla_rsel(GLA_SUB, chunk))
    return o.reshape(batch * seq, WIDTH_BV), st


def _gla_sample_body(q_ref, k_ref, g_ref, v_ref, s0_ref, rsel_ref, o_ref, s1_ref, *scratch, chunk):
    def put(j, s):
        s1_ref[j] = s

    _gla_rows(q_ref[...], k_ref[...], g_ref[...], v_ref[...], chunk, lambda j: s0_ref[j], put, o_ref,
              rsel_ref, *scratch)


def _gla_sample(q, k, g, v, st0, *, t_new, rows=128):
    t = q.shape[0]
    rows = min(rows, t)
    nb = rows // t_new
    blk = lambda w: pl.BlockSpec((rows, w), lambda i: (i, 0))
    sblk = pl.BlockSpec((nb, WIDTH_BV, WIDTH_BK), lambda i: (i, 0, 0))
    return pl.pallas_call(
        functools.partial(_gla_sample_body, chunk=t_new),
        grid=(t // rows,),
        in_specs=[blk(WIDTH_BK), blk(WIDTH_BK), blk(WIDTH_BK), blk(WIDTH_BV), sblk,
                  _const_spec((GLA_SUB, t_new * GLA_SUB))],
        out_specs=[blk(WIDTH_BV), sblk],
        out_shape=[jax.ShapeDtypeStruct((t, WIDTH_BV), F32), jax.ShapeDtypeStruct(st0.shape, F32)],
        scratch_shapes=_gla_scratch(rows, t_new),
        compiler_params=_cparams(("arbitrary",)),
        name="gla_sample",
    )(q, k, g, v, st0, _gla_rsel(GLA_SUB, t_new))


def _state_to_blockdiag(s):
    b = s.shape[0]
    eye = jnp.eye(N_HEADS_B, dtype=s.dtype)
    return jnp.einsum('bhkv,hg->bhvgk', s, eye).reshape(b, WIDTH_BV, WIDTH_BK)


def _state_from_blockdiag(st):
    b = st.shape[0]
    s5 = st.reshape(b, N_HEADS_B, DV_B, N_HEADS_B, DK_B)
    idx = jnp.arange(N_HEADS_B)
    return s5[:, idx, :, idx, :].transpose(1, 0, 3, 2)


def _outproj_body(y_ref, oa_ref, ob_ref, rb_ref, gg_ref, m64_ref, wa_ref, wb_ref, o_ref):
    ob = ob_ref[...]
    ms = jnp.dot((ob * ob).astype(BF16), m64_ref[...], preferred_element_type=F32)
    rb = rb_ref[...]
    gated = ob * lax.rsqrt(ms + EPS) * gg_ref[...] * (rb * jax.nn.sigmoid(rb))
    o_ref[...] = (y_ref[...]
                  + jnp.dot(oa_ref[...], wa_ref[...], preferred_element_type=F32)
                  + jnp.dot(gated.astype(BF16), wb_ref[...], preferred_element_type=F32))


def _outproj(y, oa, ob, rb, gg, m64, wa, wb, *, tm=512):
    t, d = y.shape
    tm = min(tm, t)
    row = lambda w: pl.BlockSpec((tm, w), lambda i: (i, 0))
    return pl.pallas_call(
        _outproj_body,
        grid=(t // tm,),
        in_specs=[row(d), row(WIDTH_A), row(WIDTH_BV), row(WIDTH_BV), _const_spec((1, WIDTH_BV)),
                  _const_spec((WIDTH_BV, WIDTH_BV)), _const_spec((WIDTH_A, d)), _const_spec((WIDTH_BV, d))],
        out_specs=row(d),
        out_shape=jax.ShapeDtypeStruct((t, d), F32),
        compiler_params=_cparams(("arbitrary",)),
        name="outproj",
    )(y, oa, ob, rb, gg, m64, wa, wb)


def _head_mean_matrix(width):
    i = jnp.arange(width) // HEAD_DIM
    return jnp.where(i[:, None] == i[None, :], 1.0 / HEAD_DIM, 0.0).astype(BF16)


def kernel(x_prompt, x_sample, cache_win_k, cache_win_v, state_gla, ffn1_norm, ffn1_w1, ffn1_w3, ffn1_w2,
           mix_norm, w_in, q_norm, k_norm, rel_bias, w_gk2, b_gk, gla_norm, w_out, ffn2_norm, ffn2_w1,
           ffn2_w3, ffn2_w2):
    batch, seq, d_model = x_prompt.shape
    dec_batch, dec_seq, _ = x_sample.shape
    depth = ffn1_w1.shape[0]
    w_buf = cache_win_k.shape[2]
    dilations = tuple(sorted((d for _, d in DILATED_BRANCHES), reverse=True))
    assert seq % (Q_BLOCK * dilations[0]) == 0
    assert all(w // d == Q_BLOCK for w, d in DILATED_BRANCHES)
    assert GLA_CHUNK % dec_seq == 0 and LANES % dec_seq == 0

    assert WIDTH_BV == MXU_N and 2 * WIDTH_BK == MXU_N and WIDTH_A % MXU_N == 0
    m64_a = _head_mean_matrix(MXU_N)
    m64_b = _head_mean_matrix(WIDTH_BV)
    tile2 = lambda g: jnp.tile(g, MXU_N // HEAD_DIM)[None, :]

    idx_add = []
    for dil in dilations:
        i0, a0 = _prompt_bias_index(dil, Q_BLOCK, first=False)
        i1, a1 = _prompt_bias_index(dil, Q_BLOCK, first=True)
        idx_add.append((jnp.concatenate([i0, i1], 0), jnp.concatenate([a0, a1], 0)))
    qi = jnp.arange(dec_seq, dtype=jnp.int32)[:, None]
    idx_add.append(_sample_bias_index(w_buf - dec_seq + qi - jnp.arange(w_buf, dtype=jnp.int32)[None, :]))
    sd_idx, sd_add = _sample_bias_index(w_buf + qi - jnp.arange(LANES, dtype=jnp.int32)[None, :])
    idx_add.append((sd_idx, jnp.where(jnp.arange(LANES)[None, :] < dec_seq, sd_add, NEG_INF)))
    *stage_tbl, tbl_w, tbl_d = _bias_tables(rel_bias, idx_add)
    stage_bias = jnp.stack([t.reshape(N_HEADS_A, 2, Q_BLOCK, 2 * Q_BLOCK).transpose(1, 0, 2, 3)
                            for t in stage_tbl])
    pair_rows = lambda t: t.reshape(N_PAIRS, 2 * dec_seq, t.shape[-1])
    bias_w, bias_d = pair_rows(tbl_w), pair_rows(tbl_d)

    yp = x_prompt.reshape(batch * seq, d_model)
    ys = x_sample.reshape(dec_batch * dec_seq, d_model)
    outs = [[] for _ in range(6)]
    for l in range(depth):
        bf = lambda w: w.astype(BF16)
        f1 = (ffn1_norm[l][None, :], bf(ffn1_w1[l]), bf(ffn1_w3[l]), bf(ffn1_w2[l]))
        f2 = (ffn2_norm[l][None, :], bf(ffn2_w1[l]), bf(ffn2_w3[l]), bf(ffn2_w2[l]))
        w_pad = jnp.pad(bf(w_in[l]), ((0, 0), (0, PROJ_PAD - w_in.shape[2])))
        wgk = jnp.pad(bf(w_gk2[l]), ((0, MXU_N - GATE_RANK), (0, 0)))
        pj = (mix_norm[l][None, :], w_pad, tile2(q_norm[l]), tile2(k_norm[l]), m64_a, wgk, b_gk[l][None, :])
        op = (jnp.tile(gla_norm[l], N_HEADS_B)[None, :], m64_b, bf(w_out[l][:WIDTH_A]), bf(w_out[l][WIDTH_A:]))

        ys1 = _ffn(ys, *f1)
        n_new = dec_batch * dec_seq
        qa_s, _, _, qb_s, kb_s, vb_s, rb_s, gk_s, kt_s, vt_s = _proj(ys1, *pj, seq=n_new, n_keep=n_new)
        lane_major = lambda c: c.transpose(0, 2, 3, 1).reshape(dec_batch, N_PAIRS, LANES, w_buf)
        ck, cv = lane_major(cache_win_k[l]), lane_major(cache_win_v[l])
        sp = _sample_specs(dec_seq, w_buf)
        win_shape = jax.ShapeDtypeStruct(ck.shape, F32)

        y1, wk, p_w, p_d = _ffn_side(
            yp, *f1, functools.partial(_sample_keys, t_new=dec_seq),
            (qa_s.reshape(dec_batch, dec_seq, WIDTH_A), kt_s[0], ck, bias_w, bias_d),
            (sp['q'], sp['new'], sp['win'], _const_spec(bias_w.shape), _const_spec(bias_d.shape)),
            (sp['win'], sp['pw'], sp['pd']),
            (win_shape, jax.ShapeDtypeStruct((dec_batch,) + bias_w.shape, BF16),
             jax.ShapeDtypeStruct((dec_batch,) + bias_d.shape, BF16)),
            steps=dec_batch, side_pieces=N_PAIRS * (SHIFT_PIECES + 1))
        n_keep = min(WIN_MAX, seq)
        qa, ka, va, qb, kb, vb, rb, gk, kt, vt = _proj(y1, *pj, seq=seq, n_keep=n_keep)
        oa = _attn_prompt(qa, ka, va, stage_bias, batch=batch, seq=seq, dilations=dilations)
        ob, st = _gla_prompt(qb, kb, gk, vb, batch=batch, seq=seq)
        yp, wv, oa_s = _ffn_side(
            _outproj(y1, oa, ob, rb, *op), *f2, functools.partial(_sample_values, t_new=dec_seq),
            (p_w, p_d, vt_s[0], cv), (sp['pw'], sp['pd'], sp['new'], sp['win']), (sp['win'], sp['q']),
            (win_shape, jax.ShapeDtypeStruct((dec_batch, dec_seq, WIDTH_A), BF16)),
            steps=dec_batch, side_pieces=N_PAIRS * (SHIFT_PIECES + 1))
        keep_major = lambda a: a.reshape(batch, N_HEADS_A, HEAD_DIM, n_keep).transpose(0, 3, 1, 2)
        outs[0].append(keep_major(kt))
        outs[1].append(keep_major(vt))
        outs[2].append(_state_from_blockdiag(st))

        ob, st = _gla_sample(qb_s, kb_s, gk_s, vb_s, _state_to_blockdiag(state_gla[l]), t_new=dec_seq)
        ys = _ffn(_outproj(ys1, oa_s.reshape(dec_batch * dec_seq, WIDTH_A), ob, rb_s, *op), *f2)
        row_major = lambda w: w.reshape(dec_batch, N_HEADS_A, HEAD_DIM, w_buf).transpose(0, 3, 1, 2)
        outs[3].append(row_major(wk))
        outs[4].append(row_major(wv))
        outs[5].append(_state_from_blockdiag(st))

    return (yp.reshape(batch, seq, d_model), ys.reshape(dec_batch, dec_seq, d_model),
            *(jnp.stack(o) for o in outs))
```
